```python
import math
import jax, jax.numpy as jnp
from jax import lax
import numpy as np

D_MODEL = 2048
BATCH = 4
SEQ = 4096
DEPTH = 1

M_HEADS = 4
M_DH = D_MODEL // M_HEADS
M_WIDTH = M_HEADS * M_DH
M_CHUNK = 64
A_HEADS = 16
A_KV_HEADS = 4
A_GROUP = A_HEADS // A_KV_HEADS
A_DH = D_MODEL // A_HEADS
A_WIDTH = A_HEADS * A_DH
WINDOW = 128
A_BLOCK = WINDOW
REL_BUCKETS = 32
REL_MAX_DIST = 128
PEER_HEADS = 8
PEER_NKEYS = 128
PEER_TOPK = 16
PEER_QDIM = 256
PEER_EXPERTS = PEER_NKEYS * PEER_NKEYS
PEER_TOKEN_BLOCK = 128
EPS = 1e-6

IN_SPLITS = (M_WIDTH, M_WIDTH, M_WIDTH, M_WIDTH, 4 * M_HEADS,
             A_WIDTH, A_KV_HEADS * A_DH, A_KV_HEADS * A_DH, D_MODEL, D_MODEL)
IN_COLS = sum(IN_SPLITS)
IN_OFFSETS = tuple(int(v) for v in np.cumsum(IN_SPLITS)[:-1])

kernel_name = "hybrid_mlstm_swa_peer_block"


def rmsnorm(x, g):
    xf = x.astype(jnp.float32)
    y = xf * lax.rsqrt(jnp.mean(xf * xf, axis=-1, keepdims=True) + EPS)
    return (y * g.astype(jnp.float32)).astype(x.dtype)


def mlstm_chunkwise(q, k, v, log_i, log_f):
    B, H, S, dk = q.shape
    dv = v.shape[-1]
    nc = S // M_CHUNK

    def to_chunks(a):
        a = a.reshape((B, H, nc, M_CHUNK) + a.shape[3:])
        return jnp.moveaxis(a, 2, 0)

    causal_in_chunk = jnp.tril(jnp.ones((M_CHUNK, M_CHUNK), dtype=bool))

    def step(carry, inp):
        C, n, m = carry
        q_, k_, v_, li, lf = inp
        b = jnp.cumsum(lf, axis=-1)
        dmat = b[..., :, None] - b[..., None, :] + li[..., None, :]
        dmat = jnp.where(causal_in_chunk, dmat, -jnp.inf)
        inter = b + m[..., None]
        m_row = jnp.maximum(inter, jnp.max(dmat, axis=-1))
        w_inter = jnp.exp(inter - m_row)
        s = jnp.einsum('bhld,bhsd->bhls', q_, k_) * jnp.exp(dmat - m_row[..., None])
        num = (w_inter[..., None] * jnp.einsum('bhld,bhdv->bhlv', q_, C)
               + jnp.einsum('bhls,bhsv->bhlv', s, v_))
        den = w_inter * jnp.einsum('bhld,bhd->bhl', q_, n) + jnp.sum(s, axis=-1)
        h = num / jnp.maximum(jnp.abs(den), jnp.exp(-m_row))[..., None]
        b_last = b[..., -1]
        g = b_last[..., None] - b + li
        m_new = jnp.maximum(b_last + m, jnp.max(g, axis=-1))
        decay = jnp.exp(b_last + m - m_new)
        wk = jnp.exp(g - m_new[..., None])
        kw = k_ * wk[..., None]
        C_new = decay[..., None, None] * C + jnp.einsum('bhld,bhlv->bhdv', kw, v_)
        n_new = decay[..., None] * n + jnp.sum(kw, axis=2)
        return (C_new, n_new, m_new), h

    init = (jnp.zeros((B, H, dk, dv), jnp.float32),
            jnp.zeros((B, H, dk), jnp.float32),
            jnp.zeros((B, H), jnp.float32))
    _, hs = lax.scan(step, init, (to_chunks(q), to_chunks(k), to_chunks(v),
                                  to_chunks(log_i), to_chunks(log_f)))
    return jnp.moveaxis(hs, 0, 2).reshape(B, H, S, dv)


def t5_bucket(rel):
    half = REL_BUCKETS // 2
    max_exact = half // 2
    ret = jnp.where(rel > 0, half, 0)
    n = jnp.abs(rel)
    nf = jnp.maximum(n, 1).astype(jnp.float32)
    large = max_exact + (jnp.log(nf / max_exact) / math.log(REL_MAX_DIST / max_exact)
                         * (half - max_exact)).astype(jnp.int32)
    large = jnp.minimum(large, half - 1)
    return ret + jnp.where(n < max_exact, n, large)


def banded_window_attention(q, k, v, sink, rel_bias):
    B, S = q.shape[0], q.shape[1]
    nb = S // A_BLOCK
    qb = q.reshape(B, nb, A_BLOCK, A_KV_HEADS, A_GROUP, A_DH)
    pad = ((0, 0), (A_BLOCK, A_BLOCK), (0, 0), (0, 0))

    def band(t):
        tp = jnp.pad(t, pad).reshape(B, nb + 2, A_BLOCK, A_KV_HEADS, A_DH)
        return jnp.concatenate([tp[:, :-2], tp[:, 1:-1], tp[:, 2:]], axis=2)

    k_band, v_band = band(k), band(v)
    q_local = jnp.arange(A_BLOCK)
    k_local = jnp.arange(3 * A_BLOCK) - A_BLOCK
    rel = k_local[None, :] - q_local[:, None]
    in_window = jnp.abs(rel) <= WINDOW
    bias = rel_bias.astype(jnp.float32)[t5_bucket(rel)]
    bias = bias.transpose(2, 0, 1).reshape(A_KV_HEADS, A_GROUP, A_BLOCK, 3 * A_BLOCK)
    k_pos = jnp.arange(nb)[:, None] * A_BLOCK + k_local[None, :]
    valid = (k_pos >= 0) & (k_pos < S)
    mask = in_window[None] & valid[:, None, :]
    sink_f = sink.astype(jnp.float32).reshape(A_KV_HEADS, A_GROUP, 1, 1)

    def block_fn(args):
        qb_, kb_, vb_, mb_ = args
        s = jnp.einsum('bqhgd,bkhd->bhgqk', qb_, kb_).astype(jnp.float32) + bias
        s = jnp.where(mb_, s, -jnp.inf)
        sink_col = jnp.broadcast_to(sink_f, s.shape[:-1] + (1,))
        p = jax.nn.softmax(jnp.concatenate([s, sink_col], axis=-1), axis=-1)[..., :-1]
        return jnp.einsum('bhgqk,bkhd->bqhgd', p.astype(vb_.dtype), vb_)

    out = lax.map(block_fn, (jnp.moveaxis(qb, 1, 0), jnp.moveaxis(k_band, 1, 0),
                             jnp.moveaxis(v_band, 1, 0), mask))
    return jnp.moveaxis(out, 0, 1).reshape(B, S, A_WIDTH)


def peer_ffn(xt, peer_wq, peer_keys, peer_u, peer_v):
    T = xt.shape[0]
    K = PEER_TOPK
    pq = (xt @ peer_wq).reshape(T, PEER_HEADS, 2, PEER_QDIM // 2)
    sub = jnp.einsum('thpd,hpnd->thpn', pq, peer_keys).astype(jnp.float32)
    sv, si = lax.top_k(sub, K)
    cand = (sv[:, :, 0, :, None] + sv[:, :, 1, None, :]).reshape(T, PEER_HEADS, K * K)
    cv, ci = lax.top_k(cand, K)
    e1 = jnp.take_along_axis(si[:, :, 0], ci // K, axis=-1)
    e2 = jnp.take_along_axis(si[:, :, 1], ci % K, axis=-1)
    experts = (e1 * PEER_NKEYS + e2).reshape(T, PEER_HEADS * K)
    gates = jax.nn.softmax(cv, axis=-1).reshape(T, PEER_HEADS * K).astype(xt.dtype)
    nblk = T // PEER_TOKEN_BLOCK

    def expert_block(args):
        xb, eb, gb = args
        u = peer_u[eb]
        act = jax.nn.gelu(jnp.einsum('tkd,td->tk', u, xb), approximate=False) * gb
        return jnp.einsum('tk,tkd->td', act, peer_v[eb])

    out = lax.map(expert_block, (xt.reshape(nblk, PEER_TOKEN_BLOCK, -1),
                                 experts.reshape(nblk, PEER_TOKEN_BLOCK, -1),
                                 gates.reshape(nblk, PEER_TOKEN_BLOCK, -1)))
    return out.reshape(T, -1)


def setup_inputs(seed: int = 0) -> dict:
    key = jax.random.key(seed)
    ks = jax.random.split(key, 20)
    nrm = lambda k, shape, scale: jax.random.normal(k, shape, jnp.float32) * scale
    x = nrm(ks[0], (BATCH, SEQ, D_MODEL), 1.0)
    norm1_g = 1.0 + nrm(ks[1], (D_MODEL,), 0.05)
    w_in = nrm(ks[2], (D_MODEL, IN_COLS), D_MODEL ** -0.5)
    ib = nrm(ks[3], (2, M_HEADS), 0.1)
    fb = jnp.linspace(3.0, 6.0, M_HEADS, dtype=jnp.float32)[None, :] + nrm(ks[4], (2, M_HEADS), 0.1)
    mlstm_gate_b = jnp.stack([ib, fb], axis=1)
    mlstm_norm_g = 1.0 + nrm(ks[5], (M_HEADS, M_DH), 0.05)
    w_m_proj = nrm(ks[6], (M_WIDTH, D_MODEL), M_WIDTH ** -0.5)
    attn_q_norm_g = 1.0 + nrm(ks[7], (A_DH,), 0.05)
    attn_k_norm_g = 1.0 + nrm(ks[8], (A_DH,), 0.05)
    attn_sink = nrm(ks[9], (A_HEADS,), 0.5)
    rel_bias = nrm(ks[10], (REL_BUCKETS, A_HEADS), 0.5)
    w_a_proj = nrm(ks[11], (A_WIDTH, D_MODEL), A_WIDTH ** -0.5)
    w_out = nrm(ks[12], (D_MODEL, D_MODEL), D_MODEL ** -0.5)
    norm2_g = 1.0 + nrm(ks[13], (D_MODEL,), 0.05)
    peer_wq = nrm(ks[14], (D_MODEL, PEER_HEADS * PEER_QDIM), D_MODEL ** -0.5)
    peer_keys = nrm(ks[15], (PEER_HEADS, 2, PEER_NKEYS, PEER_QDIM // 2), (PEER_QDIM // 2) ** -0.5)
    peer_u = nrm(ks[16], (PEER_EXPERTS, D_MODEL), D_MODEL ** -0.5)
    peer_v = nrm(ks[17], (PEER_EXPERTS, D_MODEL), PEER_HEADS ** -0.5)
    return {"x": x, "norm1_g": norm1_g, "w_in": w_in, "mlstm_gate_b": mlstm_gate_b,
            "mlstm_norm_g": mlstm_norm_g, "w_m_proj": w_m_proj,
            "attn_q_norm_g": attn_q_norm_g, "attn_k_norm_g": attn_k_norm_g,
            "attn_sink": attn_sink, "rel_bias": rel_bias, "w_a_proj": w_a_proj,
            "w_out": w_out, "norm2_g": norm2_g, "peer_wq": peer_wq,
            "peer_keys": peer_keys, "peer_u": peer_u, "peer_v": peer_v}


def reference(x, norm1_g, w_in, mlstm_gate_b, mlstm_norm_g, w_m_proj,
              attn_q_norm_g, attn_k_norm_g, attn_sink, rel_bias, w_a_proj,
              w_out, norm2_g, peer_wq, peer_keys, peer_u, peer_v):
    B, S, D = x.shape
    f32 = jnp.float32
    for _layer in range(DEPTH):
        xn = rmsnorm(x, norm1_g)
        proj = xn @ w_in
        mq, mk, mv, mo, mg, aq, ak, av, gm, ga = jnp.split(proj, IN_OFFSETS, axis=-1)

        heads_first = lambda t: t.reshape(B, S, M_HEADS, M_DH).transpose(0, 2, 1, 3).astype(f32)
        q_m = heads_first(mq)
        k_m = heads_first(mk) * (M_DH ** -0.5)
        v_m = heads_first(mv)
        gates = (mg.reshape(B, S, 2, 2, M_HEADS).astype(f32) + mlstm_gate_b.astype(f32))
        gates = gates.transpose(2, 3, 0, 4, 1)
        h_fwd = mlstm_chunkwise(q_m, k_m, v_m, gates[0, 0], jax.nn.log_sigmoid(gates[0, 1]))
        flip = lambda t: jnp.flip(t, axis=2)
        h_bwd = flip(mlstm_chunkwise(flip(q_m), flip(k_m), flip(v_m),
                                     flip(gates[1, 0]), flip(jax.nn.log_sigmoid(gates[1, 1]))))
        h_m = (h_fwd + h_bwd).transpose(0, 2, 1, 3)
        h_m = h_m * lax.rsqrt(jnp.mean(h_m * h_m, axis=-1, keepdims=True) + EPS) * mlstm_norm_g.astype(f32)
        h_m = (h_m.reshape(B, S, M_WIDTH) * jax.nn.sigmoid(mo.astype(f32))).astype(x.dtype)
        br_m = h_m @ w_m_proj

        q_a = rmsnorm(aq.reshape(B, S, A_HEADS, A_DH), attn_q_norm_g) * (A_DH ** -0.5)
        k_a = rmsnorm(ak.reshape(B, S, A_KV_HEADS, A_DH), attn_k_norm_g)
        v_a = av.reshape(B, S, A_KV_HEADS, A_DH)
        h_a = banded_window_attention(q_a, k_a, v_a, attn_sink, rel_bias)
        br_a = h_a @ w_a_proj

        merged = (jax.nn.sigmoid(gm.astype(f32)) * br_m.astype(f32)
                  + jax.nn.sigmoid(ga.astype(f32)) * br_a.astype(f32)).astype(x.dtype)
        x = x + merged @ w_out

        xn2 = rmsnorm(x, norm2_g).reshape(B * S, D)
        x = x + peer_ffn(xn2, peer_wq, peer_keys, peer_u, peer_v).reshape(B, S, D)
    return x
```

```python
import functools
import math

import jax
import jax.numpy as jnp
from jax import lax
from jax.experimental import pallas as pl
from jax.experimental.pallas import tpu as pltpu

F32 = jnp.float32
BF16 = jnp.bfloat16
I32 = jnp.int32
HIGHEST = lax.Precision.HIGHEST
EPS = 1e-6
LANES = 128
SUBLANES = 8
VMEM_LIMIT = 56 * 1024 * 1024

M_HEADS = 4
A_HEADS = 16
A_KV_HEADS = 4
A_GROUP = A_HEADS // A_KV_HEADS
A_DH = 128
WINDOW = 128
REL_BUCKETS = 32
REL_MAX_DIST = 128
PEER_HEADS = 8
PEER_NKEYS = 128
PEER_TOPK = 16
GATE_PAD = 128


def _params(sem, vmem=VMEM_LIMIT):
    return pltpu.CompilerParams(dimension_semantics=sem, vmem_limit_bytes=vmem)


def _log_sigmoid(x):
    return jnp.minimum(x, 0.0) - jnp.log1p(jnp.exp(-jnp.abs(x)))


def _in_proj_kernel(x_ref, g_ref, w_ref, wg_ref, wgt_ref, o_ref, og_ref, ogt_ref, xn_ref):
    @pl.when(pl.program_id(1) == 0)
    def _():
        x = x_ref[...]
        y = x * lax.rsqrt(jnp.mean(x * x, axis=-1, keepdims=True) + EPS) * g_ref[...]
        xn_ref[...] = y.astype(BF16)
        og_ref[...] = jnp.dot(y, wg_ref[...], precision=HIGHEST, preferred_element_type=F32)
        ogt_ref[...] = lax.dot_general(wgt_ref[...], y, (((1,), (1,)), ((), ())),
                                       precision=HIGHEST, preferred_element_type=F32)

    o_ref[...] = jnp.dot(xn_ref[...], w_ref[...], preferred_element_type=F32).astype(o_ref.dtype)


def in_proj(x2, g, w, wg, wgt, *, tm, tn):
    T, D = x2.shape
    N = w.shape[1]
    ng = wgt.shape[0]
    return pl.pallas_call(
        _in_proj_kernel,
        grid=(T // tm, N // tn),
        in_specs=[
            pl.BlockSpec((tm, D), lambda i, j: (i, 0)),
            pl.BlockSpec((1, D), lambda i, j: (0, 0)),
            pl.BlockSpec((D, tn), lambda i, j: (0, j)),
            pl.BlockSpec((D, GATE_PAD), lambda i, j: (0, 0)),
            pl.BlockSpec((ng, D), lambda i, j: (0, 0)),
        ],
        out_specs=[
            pl.BlockSpec((tm, tn), lambda i, j: (i, j)),
            pl.BlockSpec((tm, GATE_PAD), lambda i, j: (i, 0)),
            pl.BlockSpec((ng, tm), lambda i, j: (0, i)),
        ],
        out_shape=[
            jax.ShapeDtypeStruct((T, N), BF16),
            jax.ShapeDtypeStruct((T, GATE_PAD), F32),
            jax.ShapeDtypeStruct((ng, T), F32),
        ],
        scratch_shapes=[pltpu.VMEM((tm, D), BF16)],
        compiler_params=_params(("parallel", "arbitrary")),
        name="in_proj",
    )(x2, g, w, wg, wgt)


def _mlstm_kernel(q_ref, k_ref, v_ref, g_ref, gt_ref, brow_ref, bcol_ref, h_ref,
                  c_ref, n_ref, m_ref, *, nh, scale):
    d = pl.program_id(0)
    hh = pl.program_id(2)
    c = pl.program_id(3)
    L = q_ref.shape[0]

    @pl.when(c == 0)
    def _():
        c_ref[...] = jnp.zeros_like(c_ref)
        n_ref[...] = jnp.zeros_like(n_ref)
        m_ref[...] = jnp.zeros_like(m_ref)

    col_i = d * 2 * nh + hh
    col_f = col_i + nh
    fwd = d == 0

    r_idx = lax.broadcasted_iota(I32, (L, L), 0)
    s_idx = lax.broadcasted_iota(I32, (L, L), 1)
    seen = (r_idx - s_idx) * (1 - 2 * d) >= 0
    tri = seen.astype(F32)

    gcol = g_ref[...] + brow_ref[...]
    lane = lax.broadcasted_iota(I32, gcol.shape, 1)
    pick = lambda a, cidx: jnp.sum(jnp.where(lane == cidx, a, 0.0), axis=1, keepdims=True)
    lf_all = _log_sigmoid(gcol)
    b_all = jnp.dot(tri, lf_all, precision=HIGHEST, preferred_element_type=F32)
    li_col = pick(gcol, col_i)
    b_col = pick(b_all, col_f)

    grow = gt_ref[...] + bcol_ref[...]
    lf_rows = _log_sigmoid(grow)
    b_rows = lax.dot_general(lf_rows, tri, (((1,), (1,)), ((), ())),
                             precision=HIGHEST, preferred_element_type=F32)
    sub = lax.broadcasted_iota(I32, grow.shape, 0)
    pick_row = lambda a, ridx: jnp.sum(jnp.where(sub == ridx, a, 0.0), axis=0, keepdims=True)
    li_row = pick_row(grow, col_i)
    b_row = pick_row(b_rows, col_f)

    m_prev = m_ref[...]
    dmat = jnp.where(seen, b_col - b_row + li_row, -jnp.inf)
    inter = b_col + m_prev
    m_row = jnp.maximum(inter, jnp.max(dmat, axis=1, keepdims=True))
    w_inter = jnp.exp(inter - m_row)

    q = q_ref[...]
    k = k_ref[...]
    v = v_ref[...]
    qk = lax.dot_general(q, k, (((1,), (1,)), ((), ())), preferred_element_type=F32)
    s = qk * (jnp.exp(dmat - m_row) * scale)
    qf = q.astype(F32)
    num = (w_inter * jnp.dot(q, c_ref[...].astype(BF16), preferred_element_type=F32)
           + jnp.dot(s.astype(BF16), v, preferred_element_type=F32))
    den = (w_inter * jnp.sum(qf * n_ref[...], axis=1, keepdims=True)
           + jnp.sum(s, axis=1, keepdims=True))
    h_ref[...] = num / jnp.maximum(jnp.abs(den), jnp.exp(-m_row))

    b_last = jnp.max(jnp.where(fwd, b_col[L - 1:L, :], b_col[0:1, :]), axis=0, keepdims=True)
    g_col = b_last - b_col + li_col
    m_new = jnp.maximum(b_last + m_prev, jnp.max(g_col, axis=0, keepdims=True))
    decay = jnp.exp(b_last + m_prev - m_new)
    wk = jnp.exp(g_col - m_new) * scale
    kw = k.astype(F32) * wk
    c_ref[...] = decay * c_ref[...] + lax.dot_general(
        kw.astype(BF16), v, (((0,), (0,)), ((), ())), preferred_element_type=F32)
    n_ref[...] = decay * n_ref[...] + jnp.sum(kw, axis=0, keepdims=True)
    m_ref[...] = m_new


def mlstm(proj, gates, gates_t, bias_row, bias_col, *, B, S, nh, dh, L):
    nc = S // L
    T = B * S
    ng = gates_t.shape[0]

    def row(d, b, c):
        return b * nc + c + d * (nc - 1 - 2 * c)

    kern = functools.partial(_mlstm_kernel, nh=nh, scale=dh ** -0.5)
    return pl.pallas_call(
        kern,
        grid=(2, B, nh, nc),
        in_specs=[
            pl.BlockSpec((L, dh), lambda d, b, h, c: (row(d, b, c), h)),
            pl.BlockSpec((L, dh), lambda d, b, h, c: (row(d, b, c), nh + h)),
            pl.BlockSpec((L, dh), lambda d, b, h, c: (row(d, b, c), 2 * nh + h)),
            pl.BlockSpec((L, GATE_PAD), lambda d, b, h, c: (row(d, b, c), 0)),
            pl.BlockSpec((ng, L), lambda d, b, h, c: (0, row(d, b, c))),
            pl.BlockSpec((1, GATE_PAD), lambda d, b, h, c: (0, 0)),
            pl.BlockSpec((ng, 1), lambda d, b, h, c: (0, 0)),
        ],
        out_specs=pl.BlockSpec((None, L, dh), lambda d, b, h, c: (d, row(d, b, c), h)),
        out_shape=jax.ShapeDtypeStruct((2, T, nh * dh), F32),
        scratch_shapes=[pltpu.VMEM((dh, dh), F32), pltpu.VMEM((1, dh), F32), pltpu.VMEM((1, 1), F32)],
        compiler_params=_params(("parallel", "parallel", "parallel", "arbitrary")),
        name="mlstm",
    )(proj, proj, proj, gates, gates_t, bias_row, bias_col)


def _attn_kernel(sink_ref, q_ref, k0_ref, k1_ref, k2_ref, v0_ref, v1_ref, v2_ref,
                 bias_ref, qg_ref, kg_ref, o_ref, *, group, dh):
    g = pl.program_id(1)
    i = pl.program_id(2)
    nb = pl.num_programs(2)
    blk = q_ref.shape[0]

    def rms(t, gain):
        t = t.astype(F32)
        return t * lax.rsqrt(jnp.mean(t * t, axis=-1, keepdims=True) + EPS) * gain

    kg = kg_ref[...]
    kcat = jnp.concatenate([rms(r[...], kg) for r in (k0_ref, k1_ref, k2_ref)], axis=0).astype(BF16)
    vcat = jnp.concatenate([v0_ref[...], v1_ref[...], v2_ref[...]], axis=0)
    lane = lax.broadcasted_iota(I32, (1, 3 * blk), 1)
    ok = ((lane >= blk) | (i > 0)) & ((lane < 2 * blk) | (i < nb - 1))
    qg = qg_ref[...] * (dh ** -0.5)
    for hq in range(group):
        q = rms(q_ref[:, hq * dh:(hq + 1) * dh], qg).astype(BF16)
        s = lax.dot_general(q, kcat, (((1,), (1,)), ((), ())), preferred_element_type=F32)
        s = jnp.where(ok, s + bias_ref[hq], -jnp.inf)
        sink = sink_ref[g * group + hq]
        m = jnp.maximum(jnp.max(s, axis=1, keepdims=True), sink)
        p = jnp.exp(s - m)
        den = jnp.sum(p, axis=1, keepdims=True) + jnp.exp(sink - m)
        o = jnp.dot(p.astype(BF16), vcat, preferred_element_type=F32) / den
        o_ref[:, hq * dh:(hq + 1) * dh] = o.astype(o_ref.dtype)


def attention(proj, sink, bias_tab, qg, kg, *, B, S, q_col, k_col, v_col, nkv, group, dh, blk):
    nb = S // blk
    T = B * S
    gw = group * dh
    assert q_col % group == 0

    def kv_spec(col0, off):
        def imap(b, g, i, sink_ref):
            j = jnp.clip(i + off, 0, nb - 1)
            return (b * nb + j, col0 + g)
        return pl.BlockSpec((blk, dh), imap)

    kern = functools.partial(_attn_kernel, group=group, dh=dh)
    grid_spec = pltpu.PrefetchScalarGridSpec(
        num_scalar_prefetch=1,
        grid=(B, nkv, nb),
        in_specs=[
            pl.BlockSpec((blk, gw), lambda b, g, i, s: (b * nb + i, q_col // group + g)),
            kv_spec(k_col, -1), kv_spec(k_col, 0), kv_spec(k_col, 1),
            kv_spec(v_col, -1), kv_spec(v_col, 0), kv_spec(v_col, 1),
            pl.BlockSpec((group, blk, 3 * blk), lambda b, g, i, s: (g, 0, 0)),
            pl.BlockSpec((1, dh), lambda b, g, i, s: (0, 0)),
            pl.BlockSpec((1, dh), lambda b, g, i, s: (0, 0)),
        ],
        out_specs=pl.BlockSpec((blk, gw), lambda b, g, i, s: (b * nb + i, g)),
    )
    return pl.pallas_call(
        kern,
        grid_spec=grid_spec,
        out_shape=jax.ShapeDtypeStruct((T, nkv * gw), BF16),
        compiler_params=_params(("parallel", "parallel", "arbitrary")),
        name="window_attn",
    )(sink, proj, proj, proj, proj, proj, proj, proj, bias_tab, qg, kg)


def _t5_bucket(rel):
    half = REL_BUCKETS // 2
    max_exact = half // 2
    ret = jnp.where(rel > 0, half, 0)
    n = jnp.abs(rel)
    nf = jnp.maximum(n, 1).astype(F32)
    large = max_exact + (jnp.log(nf / max_exact) / math.log(REL_MAX_DIST / max_exact)
                         * (half - max_exact)).astype(I32)
    large = jnp.minimum(large, half - 1)
    return ret + jnp.where(n < max_exact, n, large)


def attn_bias_table(rel_bias, blk, window):
    q_local = jnp.arange(blk)
    k_local = jnp.arange(3 * blk) - blk
    rel = k_local[None, :] - q_local[:, None]
    bias = rel_bias.astype(F32)[_t5_bucket(rel)]
    bias = jnp.where((jnp.abs(rel) <= window)[:, :, None], bias, -jnp.inf)
    return bias.transpose(2, 0, 1)


def _merge_kernel(hs_ref, mo_ref, gm_ref, ga_ref, ha_ref, x_ref, ng_ref, n2_ref,
                  wm_ref, wa_ref, wo_ref, x1_ref, xn2_ref, *, nh):
    hsum = hs_ref[0] + hs_ref[1]
    dh = hsum.shape[1] // nh
    segs = []
    for hh in range(nh):
        seg = hsum[:, hh * dh:(hh + 1) * dh]
        segs.append(seg * lax.rsqrt(jnp.mean(seg * seg, axis=-1, keepdims=True) + EPS))
    hm = jnp.concatenate(segs, axis=1) * ng_ref[...] * jax.nn.sigmoid(mo_ref[...].astype(F32))
    br_m = jnp.dot(hm.astype(BF16), wm_ref[...], preferred_element_type=F32)
    br_a = jnp.dot(ha_ref[...], wa_ref[...], preferred_element_type=F32)
    merged = (jax.nn.sigmoid(gm_ref[...].astype(F32)) * br_m
              + jax.nn.sigmoid(ga_ref[...].astype(F32)) * br_a)
    x1 = x_ref[...] + jnp.dot(merged.astype(BF16), wo_ref[...], preferred_element_type=F32)
    x1_ref[...] = x1
    xn2_ref[...] = x1 * lax.rsqrt(jnp.mean(x1 * x1, axis=-1, keepdims=True) + EPS) * n2_ref[...]


def merge(hs, proj, ha, x2, ng, n2, wm, wa, wo, *, nh, mo_col, gm_col, ga_col, tm):
    T, D = x2.shape
    W = hs.shape[2]
    const = lambda shape: pl.BlockSpec(shape, lambda i: (0,) * len(shape), pipeline_mode=pl.Buffered(1))
    kern = functools.partial(_merge_kernel, nh=nh)
    return pl.pallas_call(
        kern,
        grid=(T // tm,),
        in_specs=[
            pl.BlockSpec((2, tm, W), lambda i: (0, i, 0)),
            pl.BlockSpec((tm, W), lambda i: (i, mo_col)),
            pl.BlockSpec((tm, D), lambda i: (i, gm_col)),
            pl.BlockSpec((tm, D), lambda i: (i, ga_col)),
            pl.BlockSpec((tm, ha.shape[1]), lambda i: (i, 0)),
            pl.BlockSpec((tm, D), lambda i: (i, 0)),
            const((1, W)), const((1, D)),
            const(wm.shape), const(wa.shape), const(wo.shape),
        ],
        out_specs=[pl.BlockSpec((tm, D), lambda i: (i, 0)), pl.BlockSpec((tm, D), lambda i: (i, 0))],
        out_shape=[jax.ShapeDtypeStruct((T, D), F32), jax.ShapeDtypeStruct((T, D), F32)],
        compiler_params=_params(("parallel",)),
        name="merge",
    )(hs, proj, proj, proj, ha, x2, ng, n2, wm, wa, wo)


def _extract_top(vals, n_take, emit):
    rows = vals.shape[0]
    ridx = lax.broadcasted_iota(I32, vals.shape, 0)

    def body(kk, cur):
        mx = jnp.max(cur, axis=0, keepdims=True)
        idx = jnp.min(jnp.where(cur == mx, ridx, rows), axis=0, keepdims=True)
        hit = ridx == idx
        emit(kk, mx, idx, hit)
        return jnp.where(hit, -jnp.inf, cur)

    lax.fori_loop(0, n_take, body, vals)


def _route_kernel(x_ref, wq_ref, keys_ref, e_ref, gate_ref, sv_ref, si_ref, pe_ref, *, nkeys, topk):
    xb = x_ref[...].astype(BF16)
    pq = jnp.dot(xb, wq_ref[...], preferred_element_type=F32)
    qh = pq.shape[1] // 2
    for p in range(2):
        sub = lax.dot_general(keys_ref[p], pq[:, p * qh:(p + 1) * qh].astype(BF16),
                              (((1,), (1,)), ((), ())), preferred_element_type=F32)

        def emit(kk, mx, idx, hit, p=p):
            sv_ref[p, pl.ds(kk, 1), :] = mx
            si_ref[p, pl.ds(kk, 1), :] = idx

        _extract_top(sub, topk, emit)

    sv0, sv1 = sv_ref[0], sv_ref[1]
    si0, si1 = si_ref[0], si_ref[1]
    cand = jnp.concatenate([sv0[a:a + 1, :] + sv1 for a in range(topk)], axis=0)
    pe_ref[...] = jnp.concatenate([si0[a:a + 1, :] * nkeys + si1 for a in range(topk)], axis=0)

    def emit2(kk, mx, idx, hit):
        gate_ref[pl.ds(kk, 1), :] = mx
        e_ref[pl.ds(kk, 1), :] = jnp.max(jnp.where(hit, pe_ref[...], -1), axis=0, keepdims=True)

    _extract_top(cand, topk, emit2)
    cv = gate_ref[...]
    ex = jnp.exp(cv - cv[0:1, :])
    gate_ref[...] = ex / jnp.sum(ex, axis=0, keepdims=True)


def peer_route(xn2, wq, keys, *, topk, tm):
    T, D = xn2.shape
    nheads, _, nkeys, kd = keys.shape
    kern = functools.partial(_route_kernel, nkeys=nkeys, topk=topk)
    return pl.pallas_call(
        kern,
        grid=(T // tm, nheads),
        in_specs=[
            pl.BlockSpec((tm, D), lambda i, h: (i, 0)),
            pl.BlockSpec((D, 2 * kd), lambda i, h: (0, h)),
            pl.BlockSpec((None, 2, nkeys, kd), lambda i, h: (h, 0, 0, 0)),
        ],
        out_specs=[pl.BlockSpec((topk, tm), lambda i, h: (h, i)),
                   pl.BlockSpec((topk, tm), lambda i, h: (h, i))],
        out_shape=[jax.ShapeDtypeStruct((nheads * topk, T), I32),
                   jax.ShapeDtypeStruct((nheads * topk, T), F32)],
        scratch_shapes=[pltpu.VMEM((2, topk, tm), F32), pltpu.VMEM((2, topk, tm), I32),
                        pltpu.VMEM((topk * topk, tm), I32)],
        compiler_params=_params(("parallel", "arbitrary")),
        name="peer_route",
    )(xn2, wq, keys)


def _gelu(s):
    return 0.5 * s * (1.0 + lax.erf(s * (2.0 ** -0.5)))


def _sublane_fold(vs):
    sub = lax.broadcasted_iota(I32, (SUBLANES, LANES), 0)
    cur = [vs[j] for j in _FOLD_FEED_INV]
    step = SUBLANES // 2
    while len(cur) > 1:
        nxt = []
        for a in range(0, len(cur), 2):
            x = cur[a] + pltpu.roll(cur[a], step, axis=0)
            y = cur[a + 1] + pltpu.roll(cur[a + 1], SUBLANES - step, axis=0)
            nxt.append(jnp.where((sub & step) != 0, x, y))
        cur = nxt
        step //= 2
    return cur[0]


def _fold_feed():
    lands = [int(format(7 - j, "03b")[::-1], 2) for j in range(SUBLANES)]
    inv = [0] * SUBLANES
    for j, s in enumerate(lands):
        inv[j] = s
    return inv


_FOLD_FEED_INV = _fold_feed()


def _expert_kernel(idx_cur_ref, idx_nxt_ref, tab_ref, x_ref, xn_ref, gate_ref, o_ref,
                   gbuf, asp_ref, sem, *, nsel):
    i = pl.program_id(0)
    nblk = pl.num_programs(0)
    tb = x_ref.shape[0]
    rows = tb * nsel
    slot = i % 2

    def issue(idx_ref, dst_slot):
        def body(r, carry):
            pltpu.make_async_copy(tab_ref.at[idx_ref[r]], gbuf.at[dst_slot, r], sem.at[dst_slot]).start()
            return carry
        lax.fori_loop(0, rows, body, 0)

    @pl.when(i == 0)
    def _():
        issue(idx_cur_ref, 0)

    @pl.when(i + 1 < nblk)
    def _():
        issue(idx_nxt_ref, 1 - slot)

    pltpu.make_async_copy(tab_ref.at[pl.ds(0, rows)], gbuf.at[slot], sem.at[slot]).wait()

    hi_mask = jnp.int32(-65536)
    ngrp = nsel // SUBLANES
    gates = gate_ref[...]
    glane = lax.broadcasted_iota(I32, gates.shape, 1)
    tok0 = (i * tb) % gates.shape[1]

    def token(t, carry):
        xt = xn_ref[t]
        half = xt.shape[0] // 2
        base = t * nsel
        cols = []
        for gidx in range(ngrp):
            qs = []
            for j in range(SUBLANES):
                w = gbuf[slot, base + gidx * SUBLANES + j]
                p = lax.bitcast_convert_type(w << 16, F32) * xt
                qs.append(p[:half] + p[half:])
            cols.append(jnp.sum(_sublane_fold(qs), axis=1, keepdims=True))
        s = jnp.concatenate(cols, axis=0)
        gate = jnp.sum(jnp.where(glane == tok0 + t, gates, 0.0), axis=1, keepdims=True)
        a = _gelu(s) * gate
        asp_ref[...] = jnp.broadcast_to(a, asp_ref.shape)
        accs = [jnp.zeros(xt.shape, F32) for _ in range(4)]
        for kk in range(nsel):
            w = gbuf[slot, base + kk]
            vf = lax.bitcast_convert_type(w & hi_mask, F32)
            accs[kk % 4] = accs[kk % 4] + vf * asp_ref[kk:kk + 1, :]
        o_ref[t] = x_ref[t] + ((accs[0] + accs[1]) + (accs[2] + accs[3]))
        return carry

    lax.fori_loop(0, tb, token, 0)


def peer_experts(idx_flat, tab, x1_3, xn2_3, gate_t, *, nsel, tb):
    T, C, _ = x1_3.shape
    nblk = T // tb
    rows = tb * nsel
    gblk = min(T, LANES)
    assert gblk % tb == 0
    kern = functools.partial(_expert_kernel, nsel=nsel)
    return pl.pallas_call(
        kern,
        grid=(nblk,),
        in_specs=[
            pl.BlockSpec((rows,), lambda i: (i,), memory_space=pltpu.SMEM),
            pl.BlockSpec((rows,), lambda i: (jnp.minimum(i + 1, nblk - 1),), memory_space=pltpu.SMEM),
            pl.BlockSpec(memory_space=pl.ANY),
            pl.BlockSpec((tb, C, LANES), lambda i: (i, 0, 0)),
            pl.BlockSpec((tb, C, LANES), lambda i: (i, 0, 0)),
            pl.BlockSpec((nsel, gblk), lambda i: (0, (i * tb) // gblk)),
        ],
        out_specs=pl.BlockSpec((tb, C, LANES), lambda i: (i, 0, 0)),
        out_shape=jax.ShapeDtypeStruct((T, C, LANES), F32),
        scratch_shapes=[pltpu.VMEM((2, rows, C, LANES), I32), pltpu.VMEM((nsel, LANES), F32),
                        pltpu.SemaphoreType.DMA((2,))],
        compiler_params=_params(("arbitrary",)),
        name="peer_experts",
    )(idx_flat, idx_flat, tab, x1_3, xn2_3, gate_t)


def pack_expert_table(peer_u, peer_v):
    E, D = peer_u.shape
    ub = lax.bitcast_convert_type(peer_u.astype(BF16), jnp.uint16).astype(jnp.uint32)
    vb = lax.bitcast_convert_type(peer_v.astype(BF16), jnp.uint16).astype(jnp.uint32)
    return lax.bitcast_convert_type((vb << 16) | ub, I32).reshape(E, D // LANES, LANES)


def kernel(x, norm1_g, w_in, mlstm_gate_b, mlstm_norm_g, w_m_proj, attn_q_norm_g, attn_k_norm_g,
           attn_sink, rel_bias, w_a_proj, w_out, norm2_g, peer_wq, peer_keys, peer_u, peer_v):
    B, S, D = x.shape
    T = B * S
    mw = w_m_proj.shape[0]
    dh_m = mw // M_HEADS
    aw = w_a_proj.shape[0]
    kvw = A_KV_HEADS * A_DH
    ngate = 4 * M_HEADS

    splits = (mw, mw, mw, mw, ngate, aw, kvw, kvw, D, D)
    offs = [0]
    for sz in splits:
        offs.append(offs[-1] + sz)
    seg = lambda n: w_in[:, offs[n]:offs[n + 1]]
    w_rest = jnp.concatenate([seg(0), seg(1), seg(2), seg(3), seg(8), seg(9), seg(5), seg(6), seg(7)],
                             axis=1).astype(BF16)
    wg = jnp.pad(seg(4), ((0, 0), (0, GATE_PAD - ngate)))
    wgt = seg(4).T

    x2 = x.reshape(T, D)
    proj, gates, gates_t = in_proj(x2, norm1_g.reshape(1, D), w_rest, wg, wgt, tm=512, tn=1024)

    gb = mlstm_gate_b.reshape(ngate).astype(F32)
    hs = mlstm(proj, gates, gates_t, jnp.pad(gb, (0, GATE_PAD - ngate)).reshape(1, GATE_PAD),
               gb.reshape(ngate, 1), B=B, S=S, nh=M_HEADS, dh=dh_m, L=256)

    a_col0 = (4 * mw + 2 * D) // A_DH
    ha = attention(proj, attn_sink.astype(F32), attn_bias_table(rel_bias, WINDOW, WINDOW),
                   attn_q_norm_g.reshape(1, A_DH), attn_k_norm_g.reshape(1, A_DH),
                   B=B, S=S, q_col=a_col0, k_col=a_col0 + A_HEADS, v_col=a_col0 + A_HEADS + A_KV_HEADS,
                   nkv=A_KV_HEADS, group=A_GROUP, dh=A_DH, blk=WINDOW)

    x1, xn2 = merge(hs, proj, ha, x2, mlstm_norm_g.reshape(1, mw), norm2_g.reshape(1, D),
                    w_m_proj.astype(BF16), w_a_proj.astype(BF16), w_out.astype(BF16),
                    nh=M_HEADS, mo_col=3, gm_col=(4 * mw) // D, ga_col=(4 * mw) // D + 1, tm=128)

    e_t, g_t = peer_route(xn2, peer_wq.astype(BF16), peer_keys.astype(BF16), topk=PEER_TOPK, tm=128)
    nsel = PEER_HEADS * PEER_TOPK
    idx_flat = e_t.T.reshape(T * nsel)
    C = D // LANES
    y = peer_experts(idx_flat, pack_expert_table(peer_u, peer_v), x1.reshape(T, C, LANES),
                     xn2.reshape(T, C, LANES), g_t, nsel=nsel, tb=8)
    return y.reshape(B, S, D)
```

```python
import functools
import math

import jax
import jax.numpy as jnp
from jax import lax
from jax.experimental import pallas as pl
from jax.experimental.pallas import tpu as pltpu

F32 = jnp.float32
BF16 = jnp.bfloat16
I32 = jnp.int32
HIGHEST = lax.Precision.HIGHEST
EPS = 1e-6
LANES = 128
SUBLANES = 8
VMEM_LIMIT = 56 * 1024 * 1024

M_HEADS = 4
A_HEADS = 16
A_KV_HEADS = 4
A_GROUP = A_HEADS // A_KV_HEADS
A_DH = 128
WINDOW = 128
REL_BUCKETS = 32
REL_MAX_DIST = 128
PEER_HEADS = 8
PEER_NKEYS = 128
PEER_TOPK = 16
GATE_PAD = 128


def _params(sem, vmem=VMEM_LIMIT):
    return pltpu.CompilerParams(dimension_semantics=sem, vmem_limit_bytes=vmem)


def _log_sigmoid(x):
    return jnp.minimum(x, 0.0) - jnp.log1p(jnp.exp(-jnp.abs(x)))


def _in_proj_kernel(x_ref, g_ref, w_ref, wg_ref, wgt_ref, o_ref, og_ref, ogt_ref, xn_ref):
    @pl.when(pl.program_id(1) == 0)
    def _():
        x = x_ref[...]
        y = x * lax.rsqrt(jnp.mean(x * x, axis=-1, keepdims=True) + EPS) * g_ref[...]
        xn_ref[...] = y.astype(BF16)
        og_ref[...] = jnp.dot(y, wg_ref[...], precision=HIGHEST, preferred_element_type=F32)
        ogt_ref[...] = lax.dot_general(wgt_ref[...], y, (((1,), (1,)), ((), ())),
                                       precision=HIGHEST, preferred_element_type=F32)

    o_ref[...] = jnp.dot(xn_ref[...], w_ref[...], preferred_element_type=F32).astype(o_ref.dtype)


def in_proj(x2, g, w, wg, wgt, *, tm, tn):
    T, D = x2.shape
    N = w.shape[1]
    ng = wgt.shape[0]
    return pl.pallas_call(
        _in_proj_kernel,
        grid=(T // tm, N // tn),
        in_specs=[
            pl.BlockSpec((tm, D), lambda i, j: (i, 0)),
            pl.BlockSpec((1, D), lambda i, j: (0, 0)),
            pl.BlockSpec((D, tn), lambda i, j: (0, j)),
            pl.BlockSpec((D, GATE_PAD), lambda i, j: (0, 0)),
            pl.BlockSpec((ng, D), lambda i, j: (0, 0)),
        ],
        out_specs=[
            pl.BlockSpec((tm, tn), lambda i, j: (i, j)),
            pl.BlockSpec((tm, GATE_PAD), lambda i, j: (i, 0)),
            pl.BlockSpec((ng, tm), lambda i, j: (0, i)),
        ],
        out_shape=[
            jax.ShapeDtypeStruct((T, N), BF16),
            jax.ShapeDtypeStruct((T, GATE_PAD), F32),
            jax.ShapeDtypeStruct((ng, T), F32),
        ],
        scratch_shapes=[pltpu.VMEM((tm, D), BF16)],
        compiler_params=_params(("parallel", "arbitrary")),
        name="in_proj",
    )(x2, g, w, wg, wgt)


def _mlstm_kernel(q_ref, k_ref, v_ref, g_ref, gt_ref, brow_ref, bcol_ref, h_ref,
                  c_ref, n_ref, m_ref, *, nh, scale):
    d = pl.program_id(0)
    hh = pl.program_id(2)
    c = pl.program_id(3)
    L = q_ref.shape[0]

    @pl.when(c == 0)
    def _():
        c_ref[...] = jnp.zeros_like(c_ref)
        n_ref[...] = jnp.zeros_like(n_ref)
        m_ref[...] = jnp.zeros_like(m_ref)

    col_i = d * 2 * nh + hh
    col_f = col_i + nh
    fwd = d == 0

    r_idx = lax.broadcasted_iota(I32, (L, L), 0)
    s_idx = lax.broadcasted_iota(I32, (L, L), 1)
    seen = (r_idx - s_idx) * (1 - 2 * d) >= 0
    tri = seen.astype(F32)

    gcol = g_ref[...] + brow_ref[...]
    lane = lax.broadcasted_iota(I32, gcol.shape, 1)
    pick = lambda a, cidx: jnp.sum(jnp.where(lane == cidx, a, 0.0), axis=1, keepdims=True)
    lf_all = _log_sigmoid(gcol)
    b_all = jnp.dot(tri, lf_all, precision=HIGHEST, preferred_element_type=F32)
    li_col = pick(gcol, col_i)
    b_col = pick(b_all, col_f)

    grow = gt_ref[...] + bcol_ref[...]
    lf_rows = _log_sigmoid(grow)
    b_rows = lax.dot_general(lf_rows, tri, (((1,), (1,)), ((), ())),
                             precision=HIGHEST, preferred_element_type=F32)
    sub = lax.broadcasted_iota(I32, grow.shape, 0)
    pick_row = lambda a, ridx: jnp.sum(jnp.where(sub == ridx, a, 0.0), axis=0, keepdims=True)
    li_row = pick_row(grow, col_i)
    b_row = pick_row(b_rows, col_f)

    m_prev = m_ref[...]
    dmat = jnp.where(seen, b_col - b_row + li_row, -jnp.inf)
    inter = b_col + m_prev
    m_row = jnp.maximum(inter, jnp.max(dmat, axis=1, keepdims=True))
    w_inter = jnp.exp(inter - m_row)

    q = q_ref[...]
    k = k_ref[...]
    v = v_ref[...]
    qk = lax.dot_general(q, k, (((1,), (1,)), ((), ())), preferred_element_type=F32)
    s = qk * (jnp.exp(dmat - m_row) * scale)
    qf = q.astype(F32)
    num = (w_inter * jnp.dot(q, c_ref[...].astype(BF16), preferred_element_type=F32)
           + jnp.dot(s.astype(BF16), v, preferred_element_type=F32))
    den = (w_inter * jnp.sum(qf * n_ref[...], axis=1, keepdims=True)
           + jnp.sum(s, axis=1, keepdims=True))
    h_ref[...] = num / jnp.maximum(jnp.abs(den), jnp.exp(-m_row))

    b_last = jnp.max(jnp.where(fwd, b_col[L - 1:L, :], b_col[0:1, :]), axis=0, keepdims=True)
    g_col = b_last - b_col + li_col
    m_new = jnp.maximum(b_last + m_prev, jnp.max(g_col, axis=0, keepdims=True))
    decay = jnp.exp(b_last + m_prev - m_new)
    wk = jnp.exp(g_col - m_new) * scale
    kw = k.astype(F32) * wk
    c_ref[...] = decay * c_ref[...] + lax.dot_general(
        kw.astype(BF16), v, (((0,), (0,)), ((), ())), preferred_element_type=F32)
    n_ref[...] = decay * n_ref[...] + jnp.sum(kw, axis=0, keepdims=True)
    m_ref[...] = m_new


def mlstm(proj, gates, gates_t, bias_row, bias_col, *, B, S, nh, dh, L):
    nc = S // L
    T = B * S
    ng = gates_t.shape[0]

    def row(d, b, c):
        return b * nc + c + d * (nc - 1 - 2 * c)

    kern = functools.partial(_mlstm_kernel, nh=nh, scale=dh ** -0.5)
    return pl.pallas_call(
        kern,
        grid=(2, B, nh, nc),
        in_specs=[
            pl.BlockSpec((L, dh), lambda d, b, h, c: (row(d, b, c), h)),
            pl.BlockSpec((L, dh), lambda d, b, h, c: (row(d, b, c), nh + h)),
            pl.BlockSpec((L, dh), lambda d, b, h, c: (row(d, b, c), 2 * nh + h)),
            pl.BlockSpec((L, GATE_PAD), lambda d, b, h, c: (row(d, b, c), 0)),
            pl.BlockSpec((ng, L), lambda d, b, h, c: (0, row(d, b, c))),
            pl.BlockSpec((1, GATE_PAD), lambda d, b, h, c: (0, 0)),
            pl.BlockSpec((ng, 1), lambda d, b, h, c: (0, 0)),
        ],
        out_specs=pl.BlockSpec((None, L, dh), lambda d, b, h, c: (d, row(d, b, c), h)),
        out_shape=jax.ShapeDtypeStruct((2, T, nh * dh), F32),
        scratch_shapes=[pltpu.VMEM((dh, dh), F32), pltpu.VMEM((1, dh), F32), pltpu.VMEM((1, 1), F32)],
        compiler_params=_params(("parallel", "parallel", "parallel", "arbitrary")),
        name="mlstm",
    )(proj, proj, proj, gates, gates_t, bias_row, bias_col)


def _attn_kernel(sink_ref, q_ref, k0_ref, k1_ref, k2_ref, v0_ref, v1_ref, v2_ref,
                 bias_ref, qg_ref, kg_ref, o_ref, *, group, dh):
    g = pl.program_id(1)
    i = pl.program_id(2)
    nb = pl.num_programs(2)
    blk = q_ref.shape[0]

    def rms(t, gain):
        t = t.astype(F32)
        return t * lax.rsqrt(jnp.mean(t * t, axis=-1, keepdims=True) + EPS) * gain

    kg = kg_ref[...]
    kcat = jnp.concatenate([rms(r[...], kg) for r in (k0_ref, k1_ref, k2_ref)], axis=0).astype(BF16)
    vcat = jnp.concatenate([v0_ref[...], v1_ref[...], v2_ref[...]], axis=0)
    lane = lax.broadcasted_iota(I32, (1, 3 * blk), 1)
    ok = ((lane >= blk) | (i > 0)) & ((lane < 2 * blk) | (i < nb - 1))
    qg = qg_ref[...] * (dh ** -0.5)
    for hq in range(group):
        q = rms(q_ref[:, hq * dh:(hq + 1) * dh], qg).astype(BF16)
        s = lax.dot_general(q, kcat, (((1,), (1,)), ((), ())), preferred_element_type=F32)
        s = jnp.where(ok, s + bias_ref[hq], -jnp.inf)
        sink = sink_ref[g * group + hq]
        m = jnp.maximum(jnp.max(s, axis=1, keepdims=True), sink)
        p = jnp.exp(s - m)
        den = jnp.sum(p, axis=1, keepdims=True) + jnp.exp(sink - m)
        o = jnp.dot(p.astype(BF16), vcat, preferred_element_type=F32) / den
        o_ref[:, hq * dh:(hq + 1) * dh] = o.astype(o_ref.dtype)


def attention(proj, sink, bias_tab, qg, kg, *, B, S, q_col, k_col, v_col, nkv, group, dh, blk):
    nb = S // blk
    T = B * S
    gw = group * dh
    assert q_col % group == 0

    def kv_spec(col0, off):
        def imap(b, g, i, sink_ref):
            j = jnp.clip(i + off, 0, nb - 1)
            return (b * nb + j, col0 + g)
        return pl.BlockSpec((blk, dh), imap)

    kern = functools.partial(_attn_kernel, group=group, dh=dh)
    grid_spec = pltpu.PrefetchScalarGridSpec(
        num_scalar_prefetch=1,
        grid=(B, nkv, nb),
        in_specs=[
            pl.BlockSpec((blk, gw), lambda b, g, i, s: (b * nb + i, q_col // group + g)),
            kv_spec(k_col, -1), kv_spec(k_col, 0), kv_spec(k_col, 1),
            kv_spec(v_col, -1), kv_spec(v_col, 0), kv_spec(v_col, 1),
            pl.BlockSpec((group, blk, 3 * blk), lambda b, g, i, s: (g, 0, 0)),
            pl.BlockSpec((1, dh), lambda b, g, i, s: (0, 0)),
            pl.BlockSpec((1, dh), lambda b, g, i, s: (0, 0)),
        ],
        out_specs=pl.BlockSpec((blk, gw), lambda b, g, i, s: (b * nb + i, g)),
    )
    return pl.pallas_call(
        kern,
        grid_spec=grid_spec,
        out_shape=jax.ShapeDtypeStruct((T, nkv * gw), BF16),
        compiler_params=_params(("parallel", "parallel", "arbitrary")),
        name="window_attn",
    )(sink, proj, proj, proj, proj, proj, proj, proj, bias_tab, qg, kg)


def _t5_bucket(rel):
    half = REL_BUCKETS // 2
    max_exact = half // 2
    ret = jnp.where(rel > 0, half, 0)
    n = jnp.abs(rel)
    nf = jnp.maximum(n, 1).astype(F32)
    large = max_exact + (jnp.log(nf / max_exact) / math.log(REL_MAX_DIST / max_exact)
                         * (half - max_exact)).astype(I32)
    large = jnp.minimum(large, half - 1)
    return ret + jnp.where(n < max_exact, n, large)


def attn_bias_table(rel_bias, blk, window):
    q_local = jnp.arange(blk)
    k_local = jnp.arange(3 * blk) - blk
    rel = k_local[None, :] - q_local[:, None]
    bias = rel_bias.astype(F32)[_t5_bucket(rel)]
    bias = jnp.where((jnp.abs(rel) <= window)[:, :, None], bias, -jnp.inf)
    return bias.transpose(2, 0, 1)


def _merge_kernel(hs_ref, mo_ref, gm_ref, ga_ref, ha_ref, x_ref, ng_ref, n2_ref,
                  wm_ref, wa_ref, wo_ref, x1_ref, xn2_ref, xn2b_ref, *, nh):
    hsum = hs_ref[0] + hs_ref[1]
    dh = hsum.shape[1] // nh
    segs = []
    for hh in range(nh):
        seg = hsum[:, hh * dh:(hh + 1) * dh]
        segs.append(seg * lax.rsqrt(jnp.mean(seg * seg, axis=-1, keepdims=True) + EPS))
    hm = jnp.concatenate(segs, axis=1) * ng_ref[...] * jax.nn.sigmoid(mo_ref[...].astype(F32))
    br_m = jnp.dot(hm.astype(BF16), wm_ref[...], preferred_element_type=F32)
    br_a = jnp.dot(ha_ref[...], wa_ref[...], preferred_element_type=F32)
    merged = (jax.nn.sigmoid(gm_ref[...].astype(F32)) * br_m
              + jax.nn.sigmoid(ga_ref[...].astype(F32)) * br_a)
    x1 = x_ref[...] + jnp.dot(merged.astype(BF16), wo_ref[...], preferred_element_type=F32)
    x1_ref[...] = x1
    xn2 = x1 * lax.rsqrt(jnp.mean(x1 * x1, axis=-1, keepdims=True) + EPS) * n2_ref[...]
    xn2_ref[...] = xn2
    xn2b_ref[...] = xn2.astype(BF16)


def merge(hs, proj, ha, x2, ng, n2, wm, wa, wo, *, nh, mo_col, gm_col, ga_col, tm):
    T, D = x2.shape
    W = hs.shape[2]
    const = lambda shape: pl.BlockSpec(shape, lambda i: (0,) * len(shape), pipeline_mode=pl.Buffered(1))
    kern = functools.partial(_merge_kernel, nh=nh)
    return pl.pallas_call(
        kern,
        grid=(T // tm,),
        in_specs=[
            pl.BlockSpec((2, tm, W), lambda i: (0, i, 0)),
            pl.BlockSpec((tm, W), lambda i: (i, mo_col)),
            pl.BlockSpec((tm, D), lambda i: (i, gm_col)),
            pl.BlockSpec((tm, D), lambda i: (i, ga_col)),
            pl.BlockSpec((tm, ha.shape[1]), lambda i: (i, 0)),
            pl.BlockSpec((tm, D), lambda i: (i, 0)),
            const((1, W)), const((1, D)),
            const(wm.shape), const(wa.shape), const(wo.shape),
        ],
        out_specs=[pl.BlockSpec((tm, D), lambda i: (i, 0))] * 3,
        out_shape=[jax.ShapeDtypeStruct((T, D), F32), jax.ShapeDtypeStruct((T, D), F32),
                   jax.ShapeDtypeStruct((T, D), BF16)],
        compiler_params=_params(("parallel",)),
        name="merge",
    )(hs, proj, proj, proj, ha, x2, ng, n2, wm, wa, wo)


def _extract_top(vals, n_take, emit):
    rows = vals.shape[0]
    ridx = lax.broadcasted_iota(I32, vals.shape, 0)

    def body(kk, cur):
        mx = jnp.max(cur, axis=0, keepdims=True)
        idx = jnp.min(jnp.where(cur == mx, ridx, rows), axis=0, keepdims=True)
        hit = ridx == idx
        emit(kk, mx, idx, hit)
        return jnp.where(hit, -jnp.inf, cur)

    lax.fori_loop(0, n_take, body, vals)


def _route_kernel(pq_ref, keys_ref, e_ref, gate_ref, sv_ref, si_ref, pe_ref, *, nkeys, topk):
    tm = pq_ref.shape[0]
    qh = pq_ref.shape[1] // 2
    sub = jnp.concatenate(
        [lax.dot_general(keys_ref[p], pq_ref[:, p * qh:(p + 1) * qh],
                         (((1,), (1,)), ((), ())), preferred_element_type=F32) for p in range(2)],
        axis=1)

    def emit(kk, mx, idx, hit):
        sv_ref[pl.ds(kk, 1), :] = mx
        si_ref[pl.ds(kk, 1), :] = idx

    _extract_top(sub, topk, emit)

    sv0, sv1 = sv_ref[:, :tm], sv_ref[:, tm:]
    si0, si1 = si_ref[:, :tm], si_ref[:, tm:]
    width = [topk // (a + 1) for a in range(topk)]
    pad = pe_ref.shape[0] - sum(width)
    cand = jnp.concatenate([sv0[a:a + 1, :] + sv1[:width[a], :] for a in range(topk)]
                           + [jnp.full((pad, tm), -jnp.inf, F32)], axis=0)
    pe_ref[...] = jnp.concatenate([si0[a:a + 1, :] * nkeys + si1[:width[a], :] for a in range(topk)]
                                  + [jnp.full((pad, tm), -1, I32)], axis=0)

    def emit2(kk, mx, idx, hit):
        gate_ref[pl.ds(kk, 1), :] = mx
        e_ref[pl.ds(kk, 1), :] = jnp.max(jnp.where(hit, pe_ref[...], -1), axis=0, keepdims=True)

    _extract_top(cand, topk, emit2)
    cv = gate_ref[...]
    ex = jnp.exp(cv - cv[0:1, :])
    gate_ref[...] = ex / jnp.sum(ex, axis=0, keepdims=True)


def _matmul_kernel(a_ref, w_ref, o_ref):
    o_ref[...] = jnp.dot(a_ref[...], w_ref[...], preferred_element_type=F32).astype(o_ref.dtype)


def matmul(a, w, *, out_dtype, tm, tn):
    T, K = a.shape
    N = w.shape[1]
    return pl.pallas_call(
        _matmul_kernel,
        grid=(T // tm, N // tn),
        in_specs=[pl.BlockSpec((tm, K), lambda i, j: (i, 0)), pl.BlockSpec((K, tn), lambda i, j: (0, j))],
        out_specs=pl.BlockSpec((tm, tn), lambda i, j: (i, j)),
        out_shape=jax.ShapeDtypeStruct((T, N), out_dtype),
        compiler_params=_params(("parallel", "parallel")),
        name="peer_query",
    )(a, w)


def peer_route(pq, keys, *, topk, tm):
    T = pq.shape[0]
    nheads, _, nkeys, kd = keys.shape
    kern = functools.partial(_route_kernel, nkeys=nkeys, topk=topk)
    ncand = sum(topk // (a + 1) for a in range(topk))
    ncand = -(-ncand // SUBLANES) * SUBLANES
    return pl.pallas_call(
        kern,
        grid=(T // tm, nheads),
        in_specs=[
            pl.BlockSpec((tm, 2 * kd), lambda i, h: (i, h)),
            pl.BlockSpec((None, 2, nkeys, kd), lambda i, h: (h, 0, 0, 0)),
        ],
        out_specs=[pl.BlockSpec((topk, tm), lambda i, h: (h, i)),
                   pl.BlockSpec((topk, tm), lambda i, h: (h, i))],
        out_shape=[jax.ShapeDtypeStruct((nheads * topk, T), I32),
                   jax.ShapeDtypeStruct((nheads * topk, T), F32)],
        scratch_shapes=[pltpu.VMEM((topk, 2 * tm), F32), pltpu.VMEM((topk, 2 * tm), I32),
                        pltpu.VMEM((ncand, tm), I32)],
        compiler_params=_params(("parallel", "arbitrary")),
        name="peer_route",
    )(pq, keys)


def _gelu(s):
    return 0.5 * s * (1.0 + lax.erf(s * (2.0 ** -0.5)))


def _sublane_fold(vs):
    sub = lax.broadcasted_iota(I32, (SUBLANES, LANES), 0)
    cur = [vs[j] for j in _FOLD_FEED_INV]
    step = SUBLANES // 2
    while len(cur) > 1:
        nxt = []
        for a in range(0, len(cur), 2):
            x = cur[a] + pltpu.roll(cur[a], step, axis=0)
            y = cur[a + 1] + pltpu.roll(cur[a + 1], SUBLANES - step, axis=0)
            nxt.append(jnp.where((sub & step) != 0, x, y))
        cur = nxt
        step //= 2
    return cur[0]


def _fold_feed():
    lands = [int(format(7 - j, "03b")[::-1], 2) for j in range(SUBLANES)]
    inv = [0] * SUBLANES
    for j, s in enumerate(lands):
        inv[j] = s
    return inv


_FOLD_FEED_INV = _fold_feed()


def _expert_kernel(idx_cur_ref, idx_nxt_ref, tab_ref, x_ref, xn_ref, gate_ref, o_ref,
                   gbuf_a, gbuf_b, act_ref, asp_ref, sem_a, sem_b, *, nsel, tb, n_first):
    i = pl.program_id(0)
    nstep = pl.num_programs(0)
    rows = tb * nsel
    ngrp = nsel // SUBLANES
    hi_mask = jnp.int32(-65536)
    gates = gate_ref[...]
    glane = lax.broadcasted_iota(I32, gates.shape, 1)
    tok0 = (i * 2 * tb) % gates.shape[1]

    def issue(idx_ref, idx_base, t, dst, sems, k_lo, k_hi):
        for kk in range(k_lo, k_hi):
            r = t * nsel + kk
            pltpu.make_async_copy(tab_ref.at[idx_ref[idx_base + r]], dst.at[r], sems.at[t]).start(priority=kk % 2)

    def wait_token(buf, sems, t):
        pltpu.make_async_copy(tab_ref.at[pl.ds(0, nsel)], buf.at[pl.ds(t * nsel, nsel)], sems.at[t]).wait()

    @pl.when(i == 0)
    def _():
        def body(t, carry):
            issue(idx_cur_ref, 0, t, gbuf_a, sem_a, 0, nsel)
            return carry
        lax.fori_loop(0, tb, body, 0)

    def phase(src, src_sems, dst, dst_sems, idx_ref, idx_base, tok_off):
        def score(t, acts):
            wait_token(src, src_sems, t)
            xt = xn_ref[tok_off + t]
            half = xt.shape[0] // 2
            base = t * nsel
            cols = []
            issue(idx_ref, idx_base, t, dst, dst_sems, 0, n_first)
            for gidx in range(ngrp):
                qs = []
                for j in range(SUBLANES):
                    p = lax.bitcast_convert_type(src[base + gidx * SUBLANES + j] << 16, F32) * xt
                    qs.append(p[:half] + p[half:])
                cols.append(jnp.sum(_sublane_fold(qs), axis=1, keepdims=True))
            s = jnp.concatenate(cols, axis=0)
            return jnp.where(glane == tok0 + tok_off + t, s, acts)

        scores = lax.fori_loop(0, tb, score, jnp.zeros(gates.shape, F32))
        act_ref[...] = _gelu(scores) * gates

        def mix(t, carry):
            a = jnp.sum(jnp.where(glane == tok0 + tok_off + t, act_ref[...], 0.0), axis=1, keepdims=True)
            asp_ref[...] = jnp.broadcast_to(a, asp_ref.shape)
            base = t * nsel
            xres = x_ref[tok_off + t]
            issue(idx_ref, idx_base, t, dst, dst_sems, n_first, nsel)
            accs = [jnp.zeros(x_ref.shape[1:], F32) for _ in range(4)]
            for kk in range(nsel):
                vf = lax.bitcast_convert_type(src[base + kk] & hi_mask, F32)
                accs[kk % 4] = accs[kk % 4] + vf * asp_ref[kk:kk + 1, :]
            o_ref[tok_off + t] = xres + ((accs[0] + accs[1]) + (accs[2] + accs[3]))
            return carry

        lax.fori_loop(0, tb, mix, 0)

    phase(gbuf_a, sem_a, gbuf_b, sem_b, idx_cur_ref, rows, 0)
    phase(gbuf_b, sem_b, gbuf_a, sem_a, idx_nxt_ref, 0, tb)

    @pl.when(i == nstep - 1)
    def _():
        def body(t, carry):
            wait_token(gbuf_a, sem_a, t)
            return carry
        lax.fori_loop(0, tb, body, 0)


def peer_experts(idx_flat, tab, x1_3, xn2_3, gate_t, *, nsel, tb):
    T, C, _ = x1_3.shape
    nstep = T // (2 * tb)
    rows = tb * nsel
    gblk = min(T, LANES)
    assert gblk % (2 * tb) == 0
    kern = functools.partial(_expert_kernel, nsel=nsel, tb=tb, n_first=(5 * nsel) // 8)
    return pl.pallas_call(
        kern,
        grid=(nstep,),
        in_specs=[
            pl.BlockSpec((2 * rows,), lambda i: (i,), memory_space=pltpu.SMEM),
            pl.BlockSpec((2 * rows,), lambda i: (jnp.minimum(i + 1, nstep - 1),), memory_space=pltpu.SMEM),
            pl.BlockSpec(memory_space=pl.ANY),
            pl.BlockSpec((2 * tb, C, LANES), lambda i: (i, 0, 0)),
            pl.BlockSpec((2 * tb, C, LANES), lambda i: (i, 0, 0)),
            pl.BlockSpec((nsel, gblk), lambda i: (0, (i * 2 * tb) // gblk)),
        ],
        out_specs=pl.BlockSpec((2 * tb, C, LANES), lambda i: (i, 0, 0)),
        out_shape=jax.ShapeDtypeStruct((T, C, LANES), F32),
        scratch_shapes=[pltpu.VMEM((rows, C, LANES), I32), pltpu.VMEM((rows, C, LANES), I32),
                        pltpu.VMEM((nsel, gblk), F32), pltpu.VMEM((nsel, LANES), F32),
                        pltpu.SemaphoreType.DMA((tb,)), pltpu.SemaphoreType.DMA((tb,))],
        compiler_params=_params(("arbitrary",)),
        name="peer_experts",
    )(idx_flat, idx_flat, tab, x1_3, xn2_3, gate_t)


def pack_expert_table(peer_u, peer_v):
    E, D = peer_u.shape
    ub = lax.bitcast_convert_type(peer_u.astype(BF16), jnp.uint16).astype(jnp.uint32)
    vb = lax.bitcast_convert_type(peer_v.astype(BF16), jnp.uint16).astype(jnp.uint32)
    return lax.bitcast_convert_type((vb << 16) | ub, I32).reshape(E, D // LANES, LANES)


def kernel(x, norm1_g, w_in, mlstm_gate_b, mlstm_norm_g, w_m_proj, attn_q_norm_g, attn_k_norm_g,
           attn_sink, rel_bias, w_a_proj, w_out, norm2_g, peer_wq, peer_keys, peer_u, peer_v):
    B, S, D = x.shape
    T = B * S
    mw = w_m_proj.shape[0]
    dh_m = mw // M_HEADS
    aw = w_a_proj.shape[0]
    kvw = A_KV_HEADS * A_DH
    ngate = 4 * M_HEADS

    splits = (mw, mw, mw, mw, ngate, aw, kvw, kvw, D, D)
    offs = [0]
    for sz in splits:
        offs.append(offs[-1] + sz)
    seg = lambda n: w_in[:, offs[n]:offs[n + 1]]
    w_rest = jnp.concatenate([seg(0), seg(1), seg(2), seg(3), seg(8), seg(9), seg(5), seg(6), seg(7)],
                             axis=1).astype(BF16)
    wg = jnp.pad(seg(4), ((0, 0), (0, GATE_PAD - ngate)))
    wgt = seg(4).T

    x2 = x.reshape(T, D)
    proj, gates, gates_t = in_proj(x2, norm1_g.reshape(1, D), w_rest, wg, wgt, tm=512, tn=1024)

    gb = mlstm_gate_b.reshape(ngate).astype(F32)
    hs = mlstm(proj, gates, gates_t, jnp.pad(gb, (0, GATE_PAD - ngate)).reshape(1, GATE_PAD),
               gb.reshape(ngate, 1), B=B, S=S, nh=M_HEADS, dh=dh_m, L=256)

    a_col0 = (4 * mw + 2 * D) // A_DH
    ha = attention(proj, attn_sink.astype(F32), attn_bias_table(rel_bias, WINDOW, WINDOW),
                   attn_q_norm_g.reshape(1, A_DH), attn_k_norm_g.reshape(1, A_DH),
                   B=B, S=S, q_col=a_col0, k_col=a_col0 + A_HEADS, v_col=a_col0 + A_HEADS + A_KV_HEADS,
                   nkv=A_KV_HEADS, group=A_GROUP, dh=A_DH, blk=WINDOW)

    x1, xn2, xn2b = merge(hs, proj, ha, x2, mlstm_norm_g.reshape(1, mw), norm2_g.reshape(1, D),
                    w_m_proj.astype(BF16), w_a_proj.astype(BF16), w_out.astype(BF16),
                    nh=M_HEADS, mo_col=3, gm_col=(4 * mw) // D, ga_col=(4 * mw) // D + 1, tm=128)

    pq = matmul(xn2b, peer_wq.astype(BF16), out_dtype=BF16, tm=512, tn=1024)
    e_t, g_t = peer_route(pq, peer_keys.astype(BF16), topk=PEER_TOPK, tm=128)
    nsel = PEER_HEADS * PEER_TOPK
    idx_flat = e_t.T.reshape(T * nsel)
    C = D // LANES
    y = peer_experts(idx_flat, pack_expert_table(peer_u, peer_v), x1.reshape(T, C, LANES),
                     xn2.reshape(T, C, LANES), g_t, nsel=nsel, tb=8)
    return y.reshape(B, S, D)
```

```python
import functools
import math

import jax
import jax.numpy as jnp
from jax import lax
from jax.experimental import pallas as pl
from jax.experimental.pallas import tpu as pltpu

F32 = jnp.float32
BF16 = jnp.bfloat16
I32 = jnp.int32
HIGHEST = lax.Precision.HIGHEST
EPS = 1e-6
LANES = 128
SUBLANES = 8
VMEM_LIMIT = 56 * 1024 * 1024

M_HEADS = 4
A_HEADS = 16
A_KV_HEADS = 4
A_GROUP = A_HEADS // A_KV_HEADS
A_DH = 128
WINDOW = 128
REL_BUCKETS = 32
REL_MAX_DIST = 128
PEER_HEADS = 8
PEER_NKEYS = 128
PEER_TOPK = 16


def _params(sem, vmem=VMEM_LIMIT):
    return pltpu.CompilerParams(dimension_semantics=sem, vmem_limit_bytes=vmem)


def _log_sigmoid(x):
    return jnp.minimum(x, 0.0) - jnp.log1p(jnp.exp(-jnp.abs(x)))


def _head_rms(t, gain, dh):
    parts = []
    for h in range(t.shape[1] // dh):
        seg = t[:, h * dh:(h + 1) * dh]
        parts.append(seg * lax.rsqrt(jnp.mean(seg * seg, axis=-1, keepdims=True) + EPS) * gain)
    return jnp.concatenate(parts, axis=1)


def _in_proj_kernel(x_ref, g_ref, w_ref, wgt_ref, qg_ref, kg_ref, o_ref, ogt_ref, xn_ref,
                    *, q_blk, k_blk, k_cols, dh):
    j = pl.program_id(1)

    @pl.when(j == 0)
    def _():
        x = x_ref[...]
        y = x * lax.rsqrt(jnp.mean(x * x, axis=-1, keepdims=True) + EPS) * g_ref[...]
        xn_ref[...] = y.astype(BF16)
        ogt_ref[...] = lax.dot_general(wgt_ref[...], y, (((1,), (1,)), ((), ())),
                                       precision=HIGHEST, preferred_element_type=F32)

    acc = jnp.dot(xn_ref[...], w_ref[...], preferred_element_type=F32)

    @pl.when(j < q_blk)
    def _():
        o_ref[...] = acc.astype(o_ref.dtype)

    @pl.when((j >= q_blk) & (j < k_blk))
    def _():
        o_ref[...] = _head_rms(acc, qg_ref[...] * (dh ** -0.5), dh).astype(o_ref.dtype)

    @pl.when(j == k_blk)
    def _():
        o_ref[...] = jnp.concatenate([_head_rms(acc[:, :k_cols], kg_ref[...], dh), acc[:, k_cols:]],
                                     axis=1).astype(o_ref.dtype)


def in_proj(x2, g, w, wgt, qg, kg, *, tm, tn, q_col, k_col, v_col, dh):
    T, D = x2.shape
    N = w.shape[1]
    ng = wgt.shape[0]
    assert q_col % tn == 0 and k_col % tn == 0 and N - k_col == tn and (v_col - k_col) % dh == 0
    kern = functools.partial(_in_proj_kernel, q_blk=q_col // tn, k_blk=k_col // tn,
                             k_cols=v_col - k_col, dh=dh)
    return pl.pallas_call(
        kern,
        grid=(T // tm, N // tn),
        in_specs=[
            pl.BlockSpec((tm, D), lambda i, j: (i, 0)),
            pl.BlockSpec((1, D), lambda i, j: (0, 0)),
            pl.BlockSpec((D, tn), lambda i, j: (0, j)),
            pl.BlockSpec((ng, D), lambda i, j: (0, 0)),
            pl.BlockSpec((1, dh), lambda i, j: (0, 0)),
            pl.BlockSpec((1, dh), lambda i, j: (0, 0)),
        ],
        out_specs=[
            pl.BlockSpec((tm, tn), lambda i, j: (i, j)),
            pl.BlockSpec((ng, tm), lambda i, j: (0, i)),
        ],
        out_shape=[
            jax.ShapeDtypeStruct((T, N), BF16),
            jax.ShapeDtypeStruct((ng, T), F32),
        ],
        scratch_shapes=[pltpu.VMEM((tm, D), BF16)],
        compiler_params=_params(("parallel", "arbitrary")),
        name="in_proj",
    )(x2, g, w, wgt, qg, kg)


def _mlstm_kernel(q_ref, k_ref, v_ref, gt_ref, bcol_ref, h_ref,
                  c_ref, n_ref, m_ref, *, nh, scale):
    d = pl.program_id(0)
    hh = pl.program_id(2)
    c = pl.program_id(3)
    L = q_ref.shape[0]

    @pl.when(c == 0)
    def _():
        c_ref[...] = jnp.zeros_like(c_ref)
        n_ref[...] = jnp.zeros_like(n_ref)
        m_ref[...] = jnp.zeros_like(m_ref)

    col_i = d * 2 * nh + hh
    col_f = col_i + nh
    fwd = d == 0

    r_idx = lax.broadcasted_iota(I32, (L, L), 0)
    s_idx = lax.broadcasted_iota(I32, (L, L), 1)
    seen = (r_idx - s_idx) * (1 - 2 * d) >= 0
    tri = seen.astype(F32)

    grow = gt_ref[...] + bcol_ref[...]
    lf_rows = _log_sigmoid(grow)
    b_rows = lax.dot_general(lf_rows, tri, (((1,), (1,)), ((), ())),
                             precision=HIGHEST, preferred_element_type=F32)
    sub = lax.broadcasted_iota(I32, grow.shape, 0)
    pick_row = lambda a, ridx: jnp.sum(jnp.where(sub == ridx, a, 0.0), axis=0, keepdims=True)
    li_row = pick_row(grow, col_i)
    b_row = pick_row(b_rows, col_f)
    cols = jnp.concatenate([li_row, b_row, jnp.zeros((LANES - 2, L), F32)], axis=0).T
    li_col = cols[:, 0:1]
    b_col = cols[:, 1:2]

    m_prev = m_ref[...]
    dmat = jnp.where(seen, b_col - b_row + li_row, -jnp.inf)
    inter = b_col + m_prev
    m_row = jnp.maximum(inter, jnp.max(dmat, axis=1, keepdims=True))
    w_inter = jnp.exp(inter - m_row)

    q = q_ref[...]
    k = k_ref[...]
    v = v_ref[...]
    qk = lax.dot_general(q, k, (((1,), (1,)), ((), ())), preferred_element_type=F32)
    s = qk * (jnp.exp(dmat - m_row) * scale)
    qf = q.astype(F32)
    num = (w_inter * jnp.dot(q, c_ref[...].astype(BF16), preferred_element_type=F32)
           + jnp.dot(s.astype(BF16), v, preferred_element_type=F32))
    den = (w_inter * jnp.sum(qf * n_ref[...], axis=1, keepdims=True)
           + jnp.sum(s, axis=1, keepdims=True))
    h_ref[...] = num / jnp.maximum(jnp.abs(den), jnp.exp(-m_row))

    b_last = jnp.max(jnp.where(fwd, b_col[L - 1:L, :], b_col[0:1, :]), axis=0, keepdims=True)
    g_col = b_last - b_col + li_col
    m_new = jnp.maximum(b_last + m_prev, jnp.max(g_col, axis=0, keepdims=True))
    decay = jnp.exp(b_last + m_prev - m_new)
    wk = jnp.exp(g_col - m_new) * scale
    kw = k.astype(F32) * wk
    c_ref[...] = decay * c_ref[...] + lax.dot_general(
        kw.astype(BF16), v, (((0,), (0,)), ((), ())), preferred_element_type=F32)
    n_ref[...] = decay * n_ref[...] + jnp.sum(kw, axis=0, keepdims=True)
    m_ref[...] = m_new


def mlstm(proj, gates_t, bias_col, *, B, S, nh, dh, L):
    nc = S // L
    T = B * S
    ng = gates_t.shape[0]

    def row(d, b, c):
        return b * nc + c + d * (nc - 1 - 2 * c)

    kern = functools.partial(_mlstm_kernel, nh=nh, scale=dh ** -0.5)
    return pl.pallas_call(
        kern,
        grid=(2, B, nh, nc),
        in_specs=[
            pl.BlockSpec((L, dh), lambda d, b, h, c: (row(d, b, c), h)),
            pl.BlockSpec((L, dh), lambda d, b, h, c: (row(d, b, c), nh + h)),
            pl.BlockSpec((L, dh), lambda d, b, h, c: (row(d, b, c), 2 * nh + h)),
            pl.BlockSpec((ng, L), lambda d, b, h, c: (0, row(d, b, c))),
            pl.BlockSpec((ng, 1), lambda d, b, h, c: (0, 0)),
        ],
        out_specs=pl.BlockSpec((None, L, dh), lambda d, b, h, c: (d, row(d, b, c), h)),
        out_shape=jax.ShapeDtypeStruct((2, T, nh * dh), F32),
        scratch_shapes=[pltpu.VMEM((dh, dh), F32), pltpu.VMEM((1, dh), F32), pltpu.VMEM((1, 1), F32)],
        compiler_params=_params(("parallel", "parallel", "parallel", "arbitrary")),
        name="mlstm",
    )(proj, proj, proj, gates_t, bias_col)


def _attn_kernel(sink_ref, q_ref, k0_ref, k1_ref, k2_ref, v0_ref, v1_ref, v2_ref,
                 bias_ref, o_ref, *, group, dh):
    g = pl.program_id(1)
    i = pl.program_id(2)
    nb = pl.num_programs(2)
    blk = q_ref.shape[0]

    kcat = jnp.concatenate([k0_ref[...], k1_ref[...], k2_ref[...]], axis=0)
    vcat = jnp.concatenate([v0_ref[...], v1_ref[...], v2_ref[...]], axis=0)
    lane = lax.broadcasted_iota(I32, (1, 3 * blk), 1)
    ok = ((lane >= blk) | (i > 0)) & ((lane < 2 * blk) | (i < nb - 1))
    for hq in range(group):
        s = lax.dot_general(q_ref[:, hq * dh:(hq + 1) * dh], kcat, (((1,), (1,)), ((), ())),
                            preferred_element_type=F32)
        s = jnp.where(ok, s + bias_ref[hq], -jnp.inf)
        sink = sink_ref[g * group + hq]
        m = jnp.maximum(jnp.max(s, axis=1, keepdims=True), sink)
        p = jnp.exp(s - m)
        den = jnp.sum(p, axis=1, keepdims=True) + jnp.exp(sink - m)
        o = jnp.dot(p.astype(BF16), vcat, preferred_element_type=F32) / den
        o_ref[:, hq * dh:(hq + 1) * dh] = o.astype(o_ref.dtype)


def attention(proj, sink, bias_tab, *, B, S, q_col, k_col, v_col, nkv, group, dh, blk):
    nb = S // blk
    T = B * S
    gw = group * dh
    assert q_col % group == 0

    def kv_spec(col0, off):
        def imap(b, g, i, sink_ref):
            j = jnp.clip(i + off, 0, nb - 1)
            return (b * nb + j, col0 + g)
        return pl.BlockSpec((blk, dh), imap)

    kern = functools.partial(_attn_kernel, group=group, dh=dh)
    grid_spec = pltpu.PrefetchScalarGridSpec(
        num_scalar_prefetch=1,
        grid=(B, nkv, nb),
        in_specs=[
            pl.BlockSpec((blk, gw), lambda b, g, i, s: (b * nb + i, q_col // group + g)),
            kv_spec(k_col, -1), kv_spec(k_col, 0), kv_spec(k_col, 1),
            kv_spec(v_col, -1), kv_spec(v_col, 0), kv_spec(v_col, 1),
            pl.BlockSpec((group, blk, 3 * blk), lambda b, g, i, s: (g, 0, 0)),
        ],
        out_specs=pl.BlockSpec((blk, gw), lambda b, g, i, s: (b * nb + i, g)),
    )
    return pl.pallas_call(
        kern,
        grid_spec=grid_spec,
        out_shape=jax.ShapeDtypeStruct((T, nkv * gw), BF16),
        compiler_params=_params(("parallel", "parallel", "arbitrary")),
        name="window_attn",
    )(sink, proj, proj, proj, proj, proj, proj, proj, bias_tab)


def _t5_bucket(rel):
    half = REL_BUCKETS // 2
    max_exact = half // 2
    ret = jnp.where(rel > 0, half, 0)
    n = jnp.abs(rel)
    nf = jnp.maximum(n, 1).astype(F32)
    large = max_exact + (jnp.log(nf / max_exact) / math.log(REL_MAX_DIST / max_exact)
                         * (half - max_exact)).astype(I32)
    large = jnp.minimum(large, half - 1)
    return ret + jnp.where(n < max_exact, n, large)


def attn_bias_table(rel_bias, blk, window):
    q_local = jnp.arange(blk)
    k_local = jnp.arange(3 * blk) - blk
    rel = k_local[None, :] - q_local[:, None]
    bucket = _t5_bucket(rel)
    hit = bucket[None, :, :, None] == jnp.arange(REL_BUCKETS)
    bias = jnp.sum(jnp.where(hit, rel_bias.astype(F32).T[:, None, None, :], 0.0), axis=-1)
    return jnp.where((jnp.abs(rel) <= window)[None], bias, -jnp.inf)


def _merge_kernel(hs_ref, mo_ref, gm_ref, ga_ref, ha_ref, x_ref, ng_ref, n2_ref,
                  wm_ref, wa_ref, wo_ref, x1_ref, xn2_ref, xn2b_ref, *, nh):
    hsum = hs_ref[0] + hs_ref[1]
    dh = hsum.shape[1] // nh
    segs = []
    for hh in range(nh):
        seg = hsum[:, hh * dh:(hh + 1) * dh]
        segs.append(seg * lax.rsqrt(jnp.mean(seg * seg, axis=-1, keepdims=True) + EPS))
    hm = jnp.concatenate(segs, axis=1) * ng_ref[...] * jax.nn.sigmoid(mo_ref[...].astype(F32))
    br_m = jnp.dot(hm.astype(BF16), wm_ref[...], preferred_element_type=F32)
    br_a = jnp.dot(ha_ref[...], wa_ref[...], preferred_element_type=F32)
    merged = (jax.nn.sigmoid(gm_ref[...].astype(F32)) * br_m
              + jax.nn.sigmoid(ga_ref[...].astype(F32)) * br_a)
    x1 = x_ref[...] + jnp.dot(merged.astype(BF16), wo_ref[...], preferred_element_type=F32)
    x1_ref[...] = x1
    xn2 = x1 * lax.rsqrt(jnp.mean(x1 * x1, axis=-1, keepdims=True) + EPS) * n2_ref[...]
    xn2_ref[...] = xn2
    xn2b_ref[...] = xn2.astype(BF16)


def merge(hs, proj, ha, x2, ng, n2, wm, wa, wo, *, nh, mo_col, gm_col, ga_col, tm):
    T, D = x2.shape
    W = hs.shape[2]
    const = lambda shape: pl.BlockSpec(shape, lambda i: (0,) * len(shape), pipeline_mode=pl.Buffered(1))
    kern = functools.partial(_merge_kernel, nh=nh)
    return pl.pallas_call(
        kern,
        grid=(T // tm,),
        in_specs=[
            pl.BlockSpec((2, tm, W), lambda i: (0, i, 0)),
            pl.BlockSpec((tm, W), lambda i: (i, mo_col)),
            pl.BlockSpec((tm, D), lambda i: (i, gm_col)),
            pl.BlockSpec((tm, D), lambda i: (i, ga_col)),
            pl.BlockSpec((tm, ha.shape[1]), lambda i: (i, 0)),
            pl.BlockSpec((tm, D), lambda i: (i, 0)),
            const((1, W)), const((1, D)),
            const(wm.shape), const(wa.shape), const(wo.shape),
        ],
        out_specs=[pl.BlockSpec((tm, D), lambda i: (i, 0))] * 3,
        out_shape=[jax.ShapeDtypeStruct((T, D), F32), jax.ShapeDtypeStruct((T, D), F32),
                   jax.ShapeDtypeStruct((T, D), BF16)],
        compiler_params=_params(("parallel",)),
        name="merge",
    )(hs, proj, proj, proj, ha, x2, ng, n2, wm, wa, wo)


def _extract_top(vals, n_take, emit):
    rows = vals.shape[0]
    ridx = lax.broadcasted_iota(I32, vals.shape, 0)

    def body(kk, cur):
        mx = jnp.max(cur, axis=0, keepdims=True)
        idx = jnp.min(jnp.where(cur == mx, ridx, rows), axis=0, keepdims=True)
        hit = ridx == idx
        emit(kk, mx, idx, hit)
        return jnp.where(hit, -jnp.inf, cur)

    lax.fori_loop(0, n_take, body, vals)


def _route_kernel(pq_ref, keys_ref, e_ref, gate_ref, sv_ref, si_ref, pe_ref, *, nkeys, topk):
    tm = pq_ref.shape[0]
    qh = pq_ref.shape[1] // 2
    for p in range(2):
        sub = lax.dot_general(keys_ref[p], pq_ref[:, p * qh:(p + 1) * qh],
                              (((1,), (1,)), ((), ())), preferred_element_type=F32)

        def emit(kk, mx, idx, hit, p=p):
            sv_ref[pl.ds(kk, 1), p * tm:(p + 1) * tm] = mx
            si_ref[pl.ds(kk, 1), p * tm:(p + 1) * tm] = idx

        _extract_top(sub, topk, emit)

    sv0, sv1 = sv_ref[:, :tm], sv_ref[:, tm:]
    si0, si1 = si_ref[:, :tm], si_ref[:, tm:]
    width = [topk // (a + 1) for a in range(topk)]
    pad = pe_ref.shape[0] - sum(width)
    cand = jnp.concatenate([sv0[a:a + 1, :] + sv1[:width[a], :] for a in range(topk)]
                           + [jnp.full((pad, tm), -jnp.inf, F32)], axis=0)
    pe_ref[...] = jnp.concatenate([si0[a:a + 1, :] * nkeys + si1[:width[a], :] for a in range(topk)]
                                  + [jnp.full((pad, tm), -1, I32)], axis=0)

    def emit2(kk, mx, idx, hit):
        gate_ref[pl.ds(kk, 1), :] = mx
        e_ref[pl.ds(kk, 1), :] = jnp.max(jnp.where(hit, pe_ref[...], -1), axis=0, keepdims=True)

    _extract_top(cand, topk, emit2)
    cv = gate_ref[...]
    ex = jnp.exp(cv - cv[0:1, :])
    gate_ref[...] = ex / jnp.sum(ex, axis=0, keepdims=True)


def _matmul_kernel(a_ref, w_ref, o_ref):
    o_ref[...] = jnp.dot(a_ref[...], w_ref[...], preferred_element_type=F32).astype(o_ref.dtype)


def matmul(a, w, *, out_dtype, tm, tn):
    T, K = a.shape
    N = w.shape[1]
    return pl.pallas_call(
        _matmul_kernel,
        grid=(T // tm, N // tn),
        in_specs=[pl.BlockSpec((tm, K), lambda i, j: (i, 0)), pl.BlockSpec((K, tn), lambda i, j: (0, j))],
        out_specs=pl.BlockSpec((tm, tn), lambda i, j: (i, j)),
        out_shape=jax.ShapeDtypeStruct((T, N), out_dtype),
        compiler_params=_params(("parallel", "parallel")),
        name="peer_query",
    )(a, w)


def peer_route(pq, keys, *, topk, tm):
    T = pq.shape[0]
    nheads, _, nkeys, kd = keys.shape
    kern = functools.partial(_route_kernel, nkeys=nkeys, topk=topk)
    ncand = sum(topk // (a + 1) for a in range(topk))
    ncand = -(-ncand // SUBLANES) * SUBLANES
    return pl.pallas_call(
        kern,
        grid=(T // tm, nheads),
        in_specs=[
            pl.BlockSpec((tm, 2 * kd), lambda i, h: (i, h)),
            pl.BlockSpec((None, 2, nkeys, kd), lambda i, h: (h, 0, 0, 0)),
        ],
        out_specs=[pl.BlockSpec((topk, tm), lambda i, h: (h, i)),
                   pl.BlockSpec((topk, tm), lambda i, h: (h, i))],
        out_shape=[jax.ShapeDtypeStruct((nheads * topk, T), I32),
                   jax.ShapeDtypeStruct((nheads * topk, T), F32)],
        scratch_shapes=[pltpu.VMEM((topk, 2 * tm), F32), pltpu.VMEM((topk, 2 * tm), I32),
                        pltpu.VMEM((ncand, tm), I32)],
        compiler_params=_params(("parallel", "arbitrary")),
        name="peer_route",
    )(pq, keys)


def _gelu(s):
    return 0.5 * s * (1.0 + lax.erf(s * (2.0 ** -0.5)))


def _expert_kernel(idx_cur_ref, idx_nxt_ref, tab_ref, x_ref, xn_ref, gate_ref, o_ref,
                   gbuf_a, gbuf_b, act_ref, sem_a, sem_b, *, nsel, tb, n_first):
    i = pl.program_id(0)
    nstep = pl.num_programs(0)
    rows = tb * nsel
    ngrp = nsel // SUBLANES
    nchunk = gbuf_a.shape[0]
    hi_mask = jnp.int32(-65536)
    gates = gate_ref[...]
    glane = lax.broadcasted_iota(I32, gates.shape, 1)
    tok0 = (i * 2 * tb) % gates.shape[1]

    def issue(idx_ref, idx_base, t, dst, sems, k_lo, k_hi):
        for kk in range(k_lo, k_hi):
            r = t * nsel + kk
            pltpu.make_async_copy(tab_ref.at[idx_ref[idx_base + r]], dst.at[:, pl.ds(r, 1), :],
                                  sems.at[t]).start(priority=kk % 2)

    def wait_token(buf, other, sems, t):
        rows_t = pl.ds(t * nsel, nsel)
        pltpu.make_async_copy(other.at[:, rows_t, :], buf.at[:, rows_t, :], sems.at[t]).wait()

    @pl.when(i == 0)
    def _():
        def body(t, carry):
            issue(idx_cur_ref, 0, t, gbuf_a, sem_a, 0, nsel)
            return carry
        lax.fori_loop(0, tb, body, 0)

    def phase(src, src_sems, dst, dst_sems, idx_ref, idx_base, tok_off):
        def score(t, acts):
            wait_token(src, dst, src_sems, t)
            issue(idx_ref, idx_base, t, dst, dst_sems, 0, n_first)
            xt = xn_ref[tok_off + t]
            xb = [jnp.broadcast_to(xt[c:c + 1, :], (SUBLANES, LANES)) for c in range(nchunk)]
            cols = []
            for gidx in range(ngrp):
                r0 = pl.multiple_of(t * nsel + gidx * SUBLANES, SUBLANES)
                acc = [jnp.zeros((SUBLANES, LANES), F32) for _ in range(2)]
                for c in range(nchunk):
                    uf = lax.bitcast_convert_type(src[c, pl.ds(r0, SUBLANES), :] << 16, F32)
                    acc[c % 2] = acc[c % 2] + uf * xb[c]
                cols.append(jnp.sum(acc[0] + acc[1], axis=1, keepdims=True))
            s = jnp.concatenate(cols, axis=0)
            return jnp.where(glane == tok0 + tok_off + t, s, acts)

        scores = lax.fori_loop(0, tb, score, jnp.zeros(gates.shape, F32))
        act_ref[...] = _gelu(scores) * gates

        def mix(t, carry):
            issue(idx_ref, idx_base, t, dst, dst_sems, n_first, nsel)
            a = jnp.sum(jnp.where(glane == tok0 + tok_off + t, act_ref[...], 0.0), axis=1, keepdims=True)
            accs = [jnp.zeros((SUBLANES, LANES), F32) for _ in range(nchunk)]
            for gidx in range(ngrp):
                r0 = pl.multiple_of(t * nsel + gidx * SUBLANES, SUBLANES)
                a_g = jnp.broadcast_to(a[gidx * SUBLANES:(gidx + 1) * SUBLANES, :], (SUBLANES, LANES))
                for c in range(nchunk):
                    vf = lax.bitcast_convert_type(src[c, pl.ds(r0, SUBLANES), :] & hi_mask, F32)
                    accs[c] = accs[c] + vf * a_g
            out = jnp.concatenate([jnp.sum(acc, axis=0, keepdims=True) for acc in accs], axis=0)
            o_ref[tok_off + t] = x_ref[tok_off + t] + out
            return carry

        lax.fori_loop(0, tb, mix, 0)

    phase(gbuf_a, sem_a, gbuf_b, sem_b, idx_cur_ref, rows, 0)
    phase(gbuf_b, sem_b, gbuf_a, sem_a, idx_nxt_ref, 0, tb)

    @pl.when(i == nstep - 1)
    def _():
        def body(t, carry):
            wait_token(gbuf_a, gbuf_b, sem_a, t)
            return carry
        lax.fori_loop(0, tb, body, 0)


def peer_experts(idx_flat, tab, x1_3, xn2_3, gate_t, *, nsel, tb):
    T, C, _ = x1_3.shape
    nstep = T // (2 * tb)
    rows = tb * nsel
    gblk = min(T, LANES)
    assert gblk % (2 * tb) == 0
    kern = functools.partial(_expert_kernel, nsel=nsel, tb=tb, n_first=(5 * nsel) // 8)
    return pl.pallas_call(
        kern,
        grid=(nstep,),
        in_specs=[
            pl.BlockSpec((2 * rows,), lambda i: (i,), memory_space=pltpu.SMEM),
            pl.BlockSpec((2 * rows,), lambda i: (jnp.minimum(i + 1, nstep - 1),), memory_space=pltpu.SMEM),
            pl.BlockSpec(memory_space=pl.ANY),
            pl.BlockSpec((2 * tb, C, LANES), lambda i: (i, 0, 0)),
            pl.BlockSpec((2 * tb, C, LANES), lambda i: (i, 0, 0)),
            pl.BlockSpec((nsel, gblk), lambda i: (0, (i * 2 * tb) // gblk)),
        ],
        out_specs=pl.BlockSpec((2 * tb, C, LANES), lambda i: (i, 0, 0)),
        out_shape=jax.ShapeDtypeStruct((T, C, LANES), F32),
        scratch_shapes=[pltpu.VMEM((C, rows, LANES), I32), pltpu.VMEM((C, rows, LANES), I32),
                        pltpu.VMEM((nsel, gblk), F32),
                        pltpu.SemaphoreType.DMA((tb,)), pltpu.SemaphoreType.DMA((tb,))],
        compiler_params=_params(("arbitrary",)),
        name="peer_experts",
    )(idx_flat, idx_flat, tab, x1_3, xn2_3, gate_t)


def pack_expert_table(peer_u, peer_v):
    E, D = peer_u.shape
    ub = lax.bitcast_convert_type(peer_u.astype(BF16), jnp.uint16).astype(jnp.uint32)
    vb = lax.bitcast_convert_type(peer_v.astype(BF16), jnp.uint16).astype(jnp.uint32)
    return lax.bitcast_convert_type((vb << 16) | ub, I32).reshape(E, D // LANES, 1, LANES)


def kernel(x, norm1_g, w_in, mlstm_gate_b, mlstm_norm_g, w_m_proj, attn_q_norm_g, attn_k_norm_g,
           attn_sink, rel_bias, w_a_proj, w_out, norm2_g, peer_wq, peer_keys, peer_u, peer_v):
    B, S, D = x.shape
    T = B * S
    mw = w_m_proj.shape[0]
    dh_m = mw // M_HEADS
    aw = w_a_proj.shape[0]
    kvw = A_KV_HEADS * A_DH
    ngate = 4 * M_HEADS

    splits = (mw, mw, mw, mw, ngate, aw, kvw, kvw, D, D)
    offs = [0]
    for sz in splits:
        offs.append(offs[-1] + sz)
    seg = lambda n: w_in[:, offs[n]:offs[n + 1]]
    w_rest = jnp.concatenate([seg(0), seg(1), seg(2), seg(3), seg(8), seg(9), seg(5), seg(6), seg(7)],
                             axis=1).astype(BF16)
    wgt = seg(4).T

    x2 = x.reshape(T, D)
    q_col = 4 * mw + 2 * D
    proj, gates_t = in_proj(x2, norm1_g.reshape(1, D), w_rest, wgt,
                            attn_q_norm_g.reshape(1, A_DH).astype(F32), attn_k_norm_g.reshape(1, A_DH).astype(F32),
                            tm=512, tn=1024, q_col=q_col, k_col=q_col + aw, v_col=q_col + aw + kvw, dh=A_DH)

    gb = mlstm_gate_b.reshape(ngate, 1).astype(F32)
    hs = mlstm(proj, gates_t, gb, B=B, S=S, nh=M_HEADS, dh=dh_m, L=256)

    a_col0 = q_col // A_DH
    ha = attention(proj, attn_sink.astype(F32), attn_bias_table(rel_bias, WINDOW, WINDOW),
                   B=B, S=S, q_col=a_col0, k_col=a_col0 + A_HEADS, v_col=a_col0 + A_HEADS + A_KV_HEADS,
                   nkv=A_KV_HEADS, group=A_GROUP, dh=A_DH, blk=WINDOW)

    x1, xn2, xn2b = merge(hs, proj, ha, x2, mlstm_norm_g.reshape(1, mw), norm2_g.reshape(1, D),
                    w_m_proj.astype(BF16), w_a_proj.astype(BF16), w_out.astype(BF16),
                    nh=M_HEADS, mo_col=3, gm_col=(4 * mw) // D, ga_col=(4 * mw) // D + 1, tm=128)

    pq = matmul(xn2b, peer_wq.astype(BF16), out_dtype=BF16, tm=512, tn=1024)
    e_t, g_t = peer_route(pq, peer_keys.astype(BF16), topk=PEER_TOPK, tm=256)
    nsel = PEER_HEADS * PEER_TOPK
    idx_flat = e_t.T.reshape(T * nsel)
    C = D // LANES
    y = peer_experts(idx_flat, pack_expert_table(peer_u, peer_v), x1.reshape(T, C, LANES),
                     xn2.reshape(T, C, LANES), g_t, nsel=nsel, tb=8)
    return y.reshape(B, S, D)
```

```python
import functools
import math

import jax
import jax.numpy as jnp
from jax import lax
from jax.experimental import pallas as pl
from jax.experimental.pallas import tpu as pltpu

F32 = jnp.float32
BF16 = jnp.bfloat16
I32 = jnp.int32
HIGHEST = lax.Precision.HIGHEST
EPS = 1e-6
LANES = 128
SUBLANES = 8
VMEM_LIMIT = 56 * 1024 * 1024

M_HEADS = 4
A_HEADS = 16
A_KV_HEADS = 4
A_GROUP = A_HEADS // A_KV_HEADS
A_DH = 128
WINDOW = 128
REL_BUCKETS = 32
REL_MAX_DIST = 128
PEER_HEADS = 8
PEER_NKEYS = 128
PEER_TOPK = 16


def _params(sem, vmem=VMEM_LIMIT):
    return pltpu.CompilerParams(dimension_semantics=sem, vmem_limit_bytes=vmem)


def _log_sigmoid(x):
    return jnp.minimum(x, 0.0) - jnp.log1p(jnp.exp(-jnp.abs(x)))


def _head_rms(t, gain, dh):
    parts = []
    for h in range(t.shape[1] // dh):
        seg = t[:, h * dh:(h + 1) * dh]
        parts.append(seg * lax.rsqrt(jnp.mean(seg * seg, axis=-1, keepdims=True) + EPS) * gain)
    return jnp.concatenate(parts, axis=1)


def _in_proj_kernel(x_ref, g_ref, w_ref, wgt_ref, qg_ref, kg_ref, o_ref, ogt_ref, xn_ref,
                    *, q_blk, k_blk, k_cols, dh, norm_rows):
    j = pl.program_id(1)

    @pl.when(j == 0)
    def _():
        for r0 in range(0, x_ref.shape[0], norm_rows):
            rows = slice(r0, r0 + norm_rows)
            x = x_ref[rows, :]
            y = x * lax.rsqrt(jnp.mean(x * x, axis=-1, keepdims=True) + EPS) * g_ref[...]
            xn_ref[rows, :] = y.astype(BF16)
            ogt_ref[:, rows] = lax.dot_general(wgt_ref[...], y, (((1,), (1,)), ((), ())),
                                               precision=HIGHEST, preferred_element_type=F32)

    acc = jnp.dot(xn_ref[...], w_ref[...], preferred_element_type=F32)

    @pl.when(j < q_blk)
    def _():
        o_ref[...] = acc.astype(o_ref.dtype)

    @pl.when((j >= q_blk) & (j < k_blk))
    def _():
        o_ref[...] = _head_rms(acc, qg_ref[...] * (dh ** -0.5), dh).astype(o_ref.dtype)

    @pl.when(j == k_blk)
    def _():
        o_ref[...] = jnp.concatenate([_head_rms(acc[:, :k_cols], kg_ref[...], dh), acc[:, k_cols:]],
                                     axis=1).astype(o_ref.dtype)


def in_proj(x2, g, w, wgt, qg, kg, *, tm, tn, q_col, k_col, v_col, dh):
    T, D = x2.shape
    N = w.shape[1]
    ng = wgt.shape[0]
    assert q_col % tn == 0 and k_col % tn == 0 and N - k_col == tn and (v_col - k_col) % dh == 0
    kern = functools.partial(_in_proj_kernel, q_blk=q_col // tn, k_blk=k_col // tn,
                             k_cols=v_col - k_col, dh=dh, norm_rows=min(tm, 256))
    return pl.pallas_call(
        kern,
        grid=(T // tm, N // tn),
        in_specs=[
            pl.BlockSpec((tm, D), lambda i, j: (i, 0)),
            pl.BlockSpec((1, D), lambda i, j: (0, 0)),
            pl.BlockSpec((D, tn), lambda i, j: (0, j)),
            pl.BlockSpec((ng, D), lambda i, j: (0, 0)),
            pl.BlockSpec((1, dh), lambda i, j: (0, 0)),
            pl.BlockSpec((1, dh), lambda i, j: (0, 0)),
        ],
        out_specs=[
            pl.BlockSpec((tm, tn), lambda i, j: (i, j)),
            pl.BlockSpec((ng, tm), lambda i, j: (0, i)),
        ],
        out_shape=[
            jax.ShapeDtypeStruct((T, N), BF16),
            jax.ShapeDtypeStruct((ng, T), F32),
        ],
        scratch_shapes=[pltpu.VMEM((tm, D), BF16)],
        compiler_params=_params(("parallel", "arbitrary")),
        name="in_proj",
    )(x2, g, w, wgt, qg, kg)


def _mlstm_kernel(q_ref, k_ref, v_ref, gt_ref, bcol_ref, h_ref,
                  c_ref, n_ref, m_ref, *, nh, scale):
    d = pl.program_id(0)
    hh = pl.program_id(2)
    c = pl.program_id(3)
    L = q_ref.shape[0]

    @pl.when(c == 0)
    def _():
        c_ref[...] = jnp.zeros_like(c_ref)
        n_ref[...] = jnp.zeros_like(n_ref)
        m_ref[...] = jnp.zeros_like(m_ref)

    col_i = d * 2 * nh + hh
    col_f = col_i + nh
    fwd = d == 0

    r_idx = lax.broadcasted_iota(I32, (L, L), 0)
    s_idx = lax.broadcasted_iota(I32, (L, L), 1)
    seen = (r_idx - s_idx) * (1 - 2 * d) >= 0
    tri = seen.astype(F32)

    grow = gt_ref[...] + bcol_ref[...]
    lf_rows = _log_sigmoid(grow)
    b_rows = lax.dot_general(lf_rows, tri, (((1,), (1,)), ((), ())),
                             precision=HIGHEST, preferred_element_type=F32)
    sub = lax.broadcasted_iota(I32, grow.shape, 0)
    pick_row = lambda a, ridx: jnp.sum(jnp.where(sub == ridx, a, 0.0), axis=0, keepdims=True)
    li_row = pick_row(grow, col_i)
    b_row = pick_row(b_rows, col_f)
    cols = jnp.concatenate([li_row, b_row, jnp.zeros((LANES - 2, L), F32)], axis=0).T
    li_col = cols[:, 0:1]
    b_col = cols[:, 1:2]

    m_prev = m_ref[...]
    dmat = jnp.where(seen, b_col - b_row + li_row, -jnp.inf)
    inter = b_col + m_prev
    m_row = jnp.maximum(inter, jnp.max(dmat, axis=1, keepdims=True))
    w_inter = jnp.exp(inter - m_row)

    q = q_ref[...]
    k = k_ref[...]
    v = v_ref[...]
    qk = lax.dot_general(q, k, (((1,), (1,)), ((), ())), preferred_element_type=F32)
    s = qk * (jnp.exp(dmat - m_row) * scale)
    qf = q.astype(F32)
    num = (w_inter * jnp.dot(q, c_ref[...].astype(BF16), preferred_element_type=F32)
           + jnp.dot(s.astype(BF16), v, preferred_element_type=F32))
    den = (w_inter * jnp.sum(qf * n_ref[...], axis=1, keepdims=True)
           + jnp.sum(s, axis=1, keepdims=True))
    h_ref[...] = num / jnp.maximum(jnp.abs(den), jnp.exp(-m_row))

    b_last = jnp.max(jnp.where(fwd, b_col[L - 1:L, :], b_col[0:1, :]), axis=0, keepdims=True)
    g_col = b_last - b_col + li_col
    m_new = jnp.maximum(b_last + m_prev, jnp.max(g_col, axis=0, keepdims=True))
    decay = jnp.exp(b_last + m_prev - m_new)
    wk = jnp.exp(g_col - m_new) * scale
    kw = k.astype(F32) * wk
    c_ref[...] = decay * c_ref[...] + lax.dot_general(
        kw.astype(BF16), v, (((0,), (0,)), ((), ())), preferred_element_type=F32)
    n_ref[...] = decay * n_ref[...] + jnp.sum(kw, axis=0, keepdims=True)
    m_ref[...] = m_new


def mlstm(proj, gates_t, bias_col, *, B, S, nh, dh, L):
    nc = S // L
    T = B * S
    ng = gates_t.shape[0]

    def row(d, b, c):
        return b * nc + c + d * (nc - 1 - 2 * c)

    kern = functools.partial(_mlstm_kernel, nh=nh, scale=dh ** -0.5)
    return pl.pallas_call(
        kern,
        grid=(2, B, nh, nc),
        in_specs=[
            pl.BlockSpec((L, dh), lambda d, b, h, c: (row(d, b, c), h)),
            pl.BlockSpec((L, dh), lambda d, b, h, c: (row(d, b, c), nh + h)),
            pl.BlockSpec((L, dh), lambda d, b, h, c: (row(d, b, c), 2 * nh + h)),
            pl.BlockSpec((ng, L), lambda d, b, h, c: (0, row(d, b, c))),
            pl.BlockSpec((ng, 1), lambda d, b, h, c: (0, 0)),
        ],
        out_specs=pl.BlockSpec((None, L, dh), lambda d, b, h, c: (d, row(d, b, c), h)),
        out_shape=jax.ShapeDtypeStruct((2, T, nh * dh), F32),
        scratch_shapes=[pltpu.VMEM((dh, dh), F32), pltpu.VMEM((1, dh), F32), pltpu.VMEM((1, 1), F32)],
        compiler_params=_params(("parallel", "parallel", "parallel", "arbitrary")),
        name="mlstm",
    )(proj, proj, proj, gates_t, bias_col)


def _attn_kernel(sink_ref, q_ref, k0_ref, k1_ref, k2_ref, v0_ref, v1_ref, v2_ref,
                 bias_ref, o_ref, *, group, dh):
    g = pl.program_id(1)
    i = pl.program_id(2)
    nb = pl.num_programs(2)
    blk = q_ref.shape[0]

    kcat = jnp.concatenate([k0_ref[...], k1_ref[...], k2_ref[...]], axis=0)
    vcat = jnp.concatenate([v0_ref[...], v1_ref[...], v2_ref[...]], axis=0)
    lane = lax.broadcasted_iota(I32, (1, 3 * blk), 1)
    ok = ((lane >= blk) | (i > 0)) & ((lane < 2 * blk) | (i < nb - 1))
    for hq in range(group):
        s = lax.dot_general(q_ref[:, hq * dh:(hq + 1) * dh], kcat, (((1,), (1,)), ((), ())),
                            preferred_element_type=F32)
        s = jnp.where(ok, s + bias_ref[hq], -jnp.inf)
        sink = sink_ref[g * group + hq]
        m = jnp.maximum(jnp.max(s, axis=1, keepdims=True), sink)
        p = jnp.exp(s - m)
        den = jnp.sum(p, axis=1, keepdims=True) + jnp.exp(sink - m)
        o = jnp.dot(p.astype(BF16), vcat, preferred_element_type=F32) / den
        o_ref[:, hq * dh:(hq + 1) * dh] = o.astype(o_ref.dtype)


def attention(proj, sink, bias_tab, *, B, S, q_col, k_col, v_col, nkv, group, dh, blk):
    nb = S // blk
    T = B * S
    gw = group * dh
    assert q_col % group == 0

    def kv_spec(col0, off):
        def imap(b, g, i, sink_ref):
            j = jnp.clip(i + off, 0, nb - 1)
            return (b * nb + j, col0 + g)
        return pl.BlockSpec((blk, dh), imap)

    kern = functools.partial(_attn_kernel, group=group, dh=dh)
    grid_spec = pltpu.PrefetchScalarGridSpec(
        num_scalar_prefetch=1,
        grid=(B, nkv, nb),
        in_specs=[
            pl.BlockSpec((blk, gw), lambda b, g, i, s: (b * nb + i, q_col // group + g)),
            kv_spec(k_col, -1), kv_spec(k_col, 0), kv_spec(k_col, 1),
            kv_spec(v_col, -1), kv_spec(v_col, 0), kv_spec(v_col, 1),
            pl.BlockSpec((group, blk, 3 * blk), lambda b, g, i, s: (g, 0, 0)),
        ],
        out_specs=pl.BlockSpec((blk, gw), lambda b, g, i, s: (b * nb + i, g)),
    )
    return pl.pallas_call(
        kern,
        grid_spec=grid_spec,
        out_shape=jax.ShapeDtypeStruct((T, nkv * gw), BF16),
        compiler_params=_params(("parallel", "parallel", "arbitrary")),
        name="window_attn",
    )(sink, proj, proj, proj, proj, proj, proj, proj, bias_tab)


def _t5_bucket(rel):
    half = REL_BUCKETS // 2
    max_exact = half // 2
    ret = jnp.where(rel > 0, half, 0)
    n = jnp.abs(rel)
    nf = jnp.maximum(n, 1).astype(F32)
    large = max_exact + (jnp.log(nf / max_exact) / math.log(REL_MAX_DIST / max_exact)
                         * (half - max_exact)).astype(I32)
    large = jnp.minimum(large, half - 1)
    return ret + jnp.where(n < max_exact, n, large)


def attn_bias_table(rel_bias, blk, window):
    q_local = jnp.arange(blk)
    k_local = jnp.arange(3 * blk) - blk
    rel = k_local[None, :] - q_local[:, None]
    bucket = _t5_bucket(rel)
    hit = bucket[None, :, :, None] == jnp.arange(REL_BUCKETS)
    bias = jnp.sum(jnp.where(hit, rel_bias.astype(F32).T[:, None, None, :], 0.0), axis=-1)
    return jnp.where((jnp.abs(rel) <= window)[None], bias, -jnp.inf)


def _merge_kernel(hs_ref, mo_ref, gm_ref, ga_ref, ha_ref, x_ref, ng_ref, n2_ref,
                  wm_ref, wa_ref, wo_ref, x1_ref, xn2_ref, xn2b_ref, *, nh):
    hsum = hs_ref[0] + hs_ref[1]
    dh = hsum.shape[1] // nh
    segs = []
    for hh in range(nh):
        seg = hsum[:, hh * dh:(hh + 1) * dh]
        segs.append(seg * lax.rsqrt(jnp.mean(seg * seg, axis=-1, keepdims=True) + EPS))
    hm = jnp.concatenate(segs, axis=1) * ng_ref[...] * jax.nn.sigmoid(mo_ref[...].astype(F32))
    br_m = jnp.dot(hm.astype(BF16), wm_ref[...], preferred_element_type=F32)
    br_a = jnp.dot(ha_ref[...], wa_ref[...], preferred_element_type=F32)
    merged = (jax.nn.sigmoid(gm_ref[...].astype(F32)) * br_m
              + jax.nn.sigmoid(ga_ref[...].astype(F32)) * br_a)
    x1 = x_ref[...] + jnp.dot(merged.astype(BF16), wo_ref[...], preferred_element_type=F32)
    x1_ref[...] = x1
    xn2 = x1 * lax.rsqrt(jnp.mean(x1 * x1, axis=-1, keepdims=True) + EPS) * n2_ref[...]
    xn2_ref[...] = xn2
    xn2b_ref[...] = xn2.astype(BF16)


def merge(hs, proj, ha, x2, ng, n2, wm, wa, wo, *, nh, mo_col, gm_col, ga_col, tm):
    T, D = x2.shape
    W = hs.shape[2]
    const = lambda shape: pl.BlockSpec(shape, lambda i: (0,) * len(shape), pipeline_mode=pl.Buffered(1))
    kern = functools.partial(_merge_kernel, nh=nh)
    return pl.pallas_call(
        kern,
        grid=(T // tm,),
        in_specs=[
            pl.BlockSpec((2, tm, W), lambda i: (0, i, 0)),
            pl.BlockSpec((tm, W), lambda i: (i, mo_col)),
            pl.BlockSpec((tm, D), lambda i: (i, gm_col)),
            pl.BlockSpec((tm, D), lambda i: (i, ga_col)),
            pl.BlockSpec((tm, ha.shape[1]), lambda i: (i, 0)),
            pl.BlockSpec((tm, D), lambda i: (i, 0)),
            const((1, W)), const((1, D)),
            const(wm.shape), const(wa.shape), const(wo.shape),
        ],
        out_specs=[pl.BlockSpec((tm, D), lambda i: (i, 0))] * 3,
        out_shape=[jax.ShapeDtypeStruct((T, D), F32), jax.ShapeDtypeStruct((T, D), F32),
                   jax.ShapeDtypeStruct((T, D), BF16)],
        compiler_params=_params(("parallel",)),
        name="merge",
    )(hs, proj, proj, proj, ha, x2, ng, n2, wm, wa, wo)


def _extract_top(vals, n_take, emit):
    rows = vals.shape[0]
    ridx = lax.broadcasted_iota(I32, vals.shape, 0)

    def body(kk, cur):
        mx = jnp.max(cur, axis=0, keepdims=True)
        idx = jnp.min(jnp.where(cur == mx, ridx, rows), axis=0, keepdims=True)
        hit = ridx == idx
        emit(kk, mx, idx, hit)
        return jnp.where(hit, -jnp.inf, cur)

    lax.fori_loop(0, n_take, body, vals)


def _route_kernel(pq_ref, keys_ref, e_ref, gate_ref, sv_ref, si_ref, pe_ref, *, nkeys, topk):
    tm = pq_ref.shape[0]
    qh = pq_ref.shape[1] // 2
    for p in range(2):
        sub = lax.dot_general(keys_ref[p], pq_ref[:, p * qh:(p + 1) * qh],
                              (((1,), (1,)), ((), ())), preferred_element_type=F32)

        def emit(kk, mx, idx, hit, p=p):
            sv_ref[pl.ds(kk, 1), p * tm:(p + 1) * tm] = mx
            si_ref[pl.ds(kk, 1), p * tm:(p + 1) * tm] = idx

        _extract_top(sub, topk, emit)

    sv0, sv1 = sv_ref[:, :tm], sv_ref[:, tm:]
    si0, si1 = si_ref[:, :tm], si_ref[:, tm:]
    width = [topk // (a + 1) for a in range(topk)]
    pad = pe_ref.shape[0] - sum(width)
    cand = jnp.concatenate([sv0[a:a + 1, :] + sv1[:width[a], :] for a in range(topk)]
                           + [jnp.full((pad, tm), -jnp.inf, F32)], axis=0)
    pe_ref[...] = jnp.concatenate([si0[a:a + 1, :] * nkeys + si1[:width[a], :] for a in range(topk)]
                                  + [jnp.full((pad, tm), -1, I32)], axis=0)

    def emit2(kk, mx, idx, hit):
        gate_ref[pl.ds(kk, 1), :] = mx
        e_ref[pl.ds(kk, 1), :] = jnp.max(jnp.where(hit, pe_ref[...], -1), axis=0, keepdims=True)

    _extract_top(cand, topk, emit2)
    cv = gate_ref[...]
    ex = jnp.exp(cv - cv[0:1, :])
    gate_ref[...] = ex / jnp.sum(ex, axis=0, keepdims=True)


def _matmul_kernel(a_ref, w_ref, o_ref):
    o_ref[...] = jnp.dot(a_ref[...], w_ref[...], preferred_element_type=F32).astype(o_ref.dtype)


def matmul(a, w, *, out_dtype, tm, tn):
    T, K = a.shape
    N = w.shape[1]
    return pl.pallas_call(
        _matmul_kernel,
        grid=(T // tm, N // tn),
        in_specs=[pl.BlockSpec((tm, K), lambda i, j: (i, 0)), pl.BlockSpec((K, tn), lambda i, j: (0, j))],
        out_specs=pl.BlockSpec((tm, tn), lambda i, j: (i, j)),
        out_shape=jax.ShapeDtypeStruct((T, N), out_dtype),
        compiler_params=_params(("parallel", "parallel")),
        name="peer_query",
    )(a, w)


def peer_route(pq, keys, *, topk, tm):
    T = pq.shape[0]
    nheads, _, nkeys, kd = keys.shape
    kern = functools.partial(_route_kernel, nkeys=nkeys, topk=topk)
    ncand = sum(topk // (a + 1) for a in range(topk))
    ncand = -(-ncand // SUBLANES) * SUBLANES
    return pl.pallas_call(
        kern,
        grid=(T // tm, nheads),
        in_specs=[
            pl.BlockSpec((tm, 2 * kd), lambda i, h: (i, h)),
            pl.BlockSpec((None, 2, nkeys, kd), lambda i, h: (h, 0, 0, 0)),
        ],
        out_specs=[pl.BlockSpec((topk, tm), lambda i, h: (h, i)),
                   pl.BlockSpec((topk, tm), lambda i, h: (h, i))],
        out_shape=[jax.ShapeDtypeStruct((nheads * topk, T), I32),
                   jax.ShapeDtypeStruct((nheads * topk, T), F32)],
        scratch_shapes=[pltpu.VMEM((topk, 2 * tm), F32), pltpu.VMEM((topk, 2 * tm), I32),
                        pltpu.VMEM((ncand, tm), I32)],
        compiler_params=_params(("parallel", "arbitrary")),
        name="peer_route",
    )(pq, keys)


def _gelu(s):
    return 0.5 * s * (1.0 + lax.erf(s * (2.0 ** -0.5)))


def _expert_kernel(idx_cur_ref, idx_nxt_ref, tab_ref, x_ref, xn_ref, gate_ref, o_ref,
                   gbuf_a, gbuf_b, act_ref, sem_a, sem_b, *, nsel, tb, n_first):
    i = pl.program_id(0)
    nstep = pl.num_programs(0)
    rows = tb * nsel
    ngrp = nsel // SUBLANES
    nchunk = gbuf_a.shape[0]
    hi_mask = jnp.int32(-65536)
    gates = gate_ref[...]
    glane = lax.broadcasted_iota(I32, gates.shape, 1)
    tok0 = (i * 2 * tb) % gates.shape[1]

    def issue(idx_ref, idx_base, t, dst, sems, k_lo, k_hi):
        tok_idx = idx_ref.at[pl.ds(idx_base + t * nsel, nsel)]
        for kk in range(k_lo, k_hi):
            pltpu.make_async_copy(tab_ref.at[tok_idx[kk]], dst.at[:, t * nsel + kk, :],
                                  sems.at[t]).start(priority=kk % 2)

    def wait_token(buf, other, sems, t):
        rows_t = pl.ds(t * nsel, nsel)
        pltpu.make_async_copy(other.at[:, rows_t, :], buf.at[:, rows_t, :], sems.at[t]).wait()

    @pl.when(i == 0)
    def _():
        def body(t, carry):
            issue(idx_cur_ref, 0, t, gbuf_a, sem_a, 0, nsel)
            return carry
        lax.fori_loop(0, tb, body, 0)

    def phase(src, src_sems, dst, dst_sems, idx_ref, idx_base, tok_off):
        def score(t, acts):
            wait_token(src, dst, src_sems, t)
            issue(idx_ref, idx_base, t, dst, dst_sems, 0, n_first)
            xt = xn_ref[tok_off + t]
            xb = [jnp.broadcast_to(xt[c:c + 1, :], (SUBLANES, LANES)) for c in range(nchunk)]
            cols = []
            for gidx in range(ngrp):
                r0 = pl.multiple_of(t * nsel + gidx * SUBLANES, SUBLANES)
                acc = [jnp.zeros((SUBLANES, LANES), F32) for _ in range(2)]
                for c in range(nchunk):
                    uf = lax.bitcast_convert_type(src[c, pl.ds(r0, SUBLANES), :] << 16, F32)
                    acc[c % 2] = acc[c % 2] + uf * xb[c]
                cols.append(jnp.sum(acc[0] + acc[1], axis=1, keepdims=True))
            s = jnp.concatenate(cols, axis=0)
            return jnp.where(glane == tok0 + tok_off + t, s, acts)

        scores = lax.fori_loop(0, tb, score, jnp.zeros(gates.shape, F32))
        act_ref[...] = _gelu(scores) * gates

        def mix(t, carry):
            issue(idx_ref, idx_base, t, dst, dst_sems, n_first, nsel)
            a = jnp.sum(jnp.where(glane == tok0 + tok_off + t, act_ref[...], 0.0), axis=1, keepdims=True)
            accs = [jnp.zeros((SUBLANES, LANES), F32) for _ in range(nchunk)]
            for gidx in range(ngrp):
                r0 = pl.multiple_of(t * nsel + gidx * SUBLANES, SUBLANES)
                a_g = jnp.broadcast_to(a[gidx * SUBLANES:(gidx + 1) * SUBLANES, :], (SUBLANES, LANES))
                for c in range(nchunk):
                    vf = lax.bitcast_convert_type(src[c, pl.ds(r0, SUBLANES), :] & hi_mask, F32)
                    accs[c] = accs[c] + vf * a_g
            out = jnp.concatenate([jnp.sum(acc, axis=0, keepdims=True) for acc in accs], axis=0)
            o_ref[tok_off + t] = x_ref[tok_off + t] + out
            return carry

        lax.fori_loop(0, tb, mix, 0)

    phase(gbuf_a, sem_a, gbuf_b, sem_b, idx_cur_ref, rows, 0)
    phase(gbuf_b, sem_b, gbuf_a, sem_a, idx_nxt_ref, 0, tb)

    @pl.when(i == nstep - 1)
    def _():
        def body(t, carry):
            wait_token(gbuf_a, gbuf_b, sem_a, t)
            return carry
        lax.fori_loop(0, tb, body, 0)


def peer_experts(idx_flat, tab, x1_3, xn2_3, gate_t, *, nsel, tb):
    T, C, _ = x1_3.shape
    nstep = T // (2 * tb)
    rows = tb * nsel
    gblk = min(T, LANES)
    assert gblk % (2 * tb) == 0
    kern = functools.partial(_expert_kernel, nsel=nsel, tb=tb, n_first=(5 * nsel) // 8)
    return pl.pallas_call(
        kern,
        grid=(nstep,),
        in_specs=[
            pl.BlockSpec((2 * rows,), lambda i: (i,), memory_space=pltpu.SMEM),
            pl.BlockSpec((2 * rows,), lambda i: (jnp.minimum(i + 1, nstep - 1),), memory_space=pltpu.SMEM),
            pl.BlockSpec(memory_space=pl.ANY),
            pl.BlockSpec((2 * tb, C, LANES), lambda i: (i, 0, 0)),
            pl.BlockSpec((2 * tb, C, LANES), lambda i: (i, 0, 0)),
            pl.BlockSpec((nsel, gblk), lambda i: (0, (i * 2 * tb) // gblk)),
        ],
        out_specs=pl.BlockSpec((2 * tb, C, LANES), lambda i: (i, 0, 0)),
        out_shape=jax.ShapeDtypeStruct((T, C, LANES), F32),
        scratch_shapes=[pltpu.VMEM((C, rows, LANES), I32), pltpu.VMEM((C, rows, LANES), I32),
                        pltpu.VMEM((nsel, gblk), F32),
                        pltpu.SemaphoreType.DMA((tb,)), pltpu.SemaphoreType.DMA((tb,))],
        compiler_params=_params(("arbitrary",)),
        name="peer_experts",
    )(idx_flat, idx_flat, tab, x1_3, xn2_3, gate_t)


def pack_expert_table(peer_u, peer_v):
    E, D = peer_u.shape
    ub = lax.bitcast_convert_type(peer_u.astype(BF16), jnp.uint16).astype(jnp.uint32)
    vb = lax.bitcast_convert_type(peer_v.astype(BF16), jnp.uint16).astype(jnp.uint32)
    return lax.bitcast_convert_type((vb << 16) | ub, I32).reshape(E, D // LANES, LANES)


def kernel(x, norm1_g, w_in, mlstm_gate_b, mlstm_norm_g, w_m_proj, attn_q_norm_g, attn_k_norm_g,
           attn_sink, rel_bias, w_a_proj, w_out, norm2_g, peer_wq, peer_keys, peer_u, peer_v):
    B, S, D = x.shape
    T = B * S
    mw = w_m_proj.shape[0]
    dh_m = mw // M_HEADS
    aw = w_a_proj.shape[0]
    kvw = A_KV_HEADS * A_DH
    ngate = 4 * M_HEADS

    splits = (mw, mw, mw, mw, ngate, aw, kvw, kvw, D, D)
    offs = [0]
    for sz in splits:
        offs.append(offs[-1] + sz)
    seg = lambda n: w_in[:, offs[n]:offs[n + 1]]
    w_rest = jnp.concatenate([seg(0), seg(1), seg(2), seg(3), seg(8), seg(9), seg(5), seg(6), seg(7)],
                             axis=1).astype(BF16)
    wgt = seg(4).T

    x2 = x.reshape(T, D)
    q_col = 4 * mw + 2 * D
    proj, gates_t = in_proj(x2, norm1_g.reshape(1, D), w_rest, wgt,
                            attn_q_norm_g.reshape(1, A_DH).astype(F32), attn_k_norm_g.reshape(1, A_DH).astype(F32),
                            tm=1024, tn=1024, q_col=q_col, k_col=q_col + aw, v_col=q_col + aw + kvw, dh=A_DH)

    gb = mlstm_gate_b.reshape(ngate, 1).astype(F32)
    hs = mlstm(proj, gates_t, gb, B=B, S=S, nh=M_HEADS, dh=dh_m, L=256)

    a_col0 = q_col // A_DH
    ha = attention(proj, attn_sink.astype(F32), attn_bias_table(rel_bias, WINDOW, WINDOW),
                   B=B, S=S, q_col=a_col0, k_col=a_col0 + A_HEADS, v_col=a_col0 + A_HEADS + A_KV_HEADS,
                   nkv=A_KV_HEADS, group=A_GROUP, dh=A_DH, blk=WINDOW)

    x1, xn2, xn2b = merge(hs, proj, ha, x2, mlstm_norm_g.reshape(1, mw), norm2_g.reshape(1, D),
                    w_m_proj.astype(BF16), w_a_proj.astype(BF16), w_out.astype(BF16),
                    nh=M_HEADS, mo_col=3, gm_col=(4 * mw) // D, ga_col=(4 * mw) // D + 1, tm=128)

    pq = matmul(xn2b, peer_wq.astype(BF16), out_dtype=BF16, tm=512, tn=1024)
    e_t, g_t = peer_route(pq, peer_keys.astype(BF16), topk=PEER_TOPK, tm=256)
    nsel = PEER_HEADS * PEER_TOPK
    idx_flat = e_t.T.reshape(T * nsel)
    C = D // LANES
    y = peer_experts(idx_flat, pack_expert_table(peer_u, peer_v), x1.reshape(T, C, LANES),
                     xn2.reshape(T, C, LANES), g_t, nsel=nsel, tb=8)
    return y.reshape(B, S, D)
```

```python
import functools
import math

import jax
import jax.numpy as jnp
from jax import lax
from jax.experimental import pallas as pl
from jax.experimental.pallas import tpu as pltpu

F32 = jnp.float32
BF16 = jnp.bfloat16
I32 = jnp.int32
HIGHEST = lax.Precision.HIGHEST
EPS = 1e-6
LANES = 128
SUBLANES = 8
VMEM_LIMIT = 56 * 1024 * 1024

M_HEADS = 4
A_HEADS = 16
A_KV_HEADS = 4
A_GROUP = A_HEADS // A_KV_HEADS
A_DH = 128
WINDOW = 128
REL_BUCKETS = 32
REL_MAX_DIST = 128
PEER_HEADS = 8
PEER_NKEYS = 128
PEER_TOPK = 16


def _params(sem, vmem=VMEM_LIMIT):
    return pltpu.CompilerParams(dimension_semantics=sem, vmem_limit_bytes=vmem)


def _log_sigmoid(x):
    return jnp.minimum(x, 0.0) - jnp.log1p(jnp.exp(-jnp.abs(x)))


def _head_rms(t, gain, dh):
    parts = []
    for h in range(t.shape[1] // dh):
        seg = t[:, h * dh:(h + 1) * dh]
        parts.append(seg * lax.rsqrt(jnp.mean(seg * seg, axis=-1, keepdims=True) + EPS) * gain)
    return jnp.concatenate(parts, axis=1)


def _in_proj_kernel(x_ref, g_ref, w_ref, wgt_ref, qg_ref, kg_ref, o_ref, ogt_ref, xn_ref,
                    *, q_blk, k_blk, k_cols, dh, norm_rows):
    j = pl.program_id(1)

    @pl.when(j == 0)
    def _():
        for r0 in range(0, x_ref.shape[0], norm_rows):
            rows = slice(r0, r0 + norm_rows)
            x = x_ref[rows, :]
            y = x * lax.rsqrt(jnp.mean(x * x, axis=-1, keepdims=True) + EPS) * g_ref[...]
            xn_ref[rows, :] = y.astype(BF16)
            ogt_ref[:, rows] = lax.dot_general(wgt_ref[...], y, (((1,), (1,)), ((), ())),
                                               precision=HIGHEST, preferred_element_type=F32)

    acc = jnp.dot(xn_ref[...], w_ref[...], preferred_element_type=F32)

    @pl.when(j < q_blk)
    def _():
        o_ref[...] = acc.astype(o_ref.dtype)

    @pl.when((j >= q_blk) & (j < k_blk))
    def _():
        o_ref[...] = _head_rms(acc, qg_ref[...] * (dh ** -0.5), dh).astype(o_ref.dtype)

    @pl.when(j == k_blk)
    def _():
        o_ref[...] = jnp.concatenate([_head_rms(acc[:, :k_cols], kg_ref[...], dh), acc[:, k_cols:]],
                                     axis=1).astype(o_ref.dtype)


def in_proj(x2, g, w, wgt, qg, kg, *, tm, tn, q_col, k_col, v_col, dh):
    T, D = x2.shape
    N = w.shape[1]
    ng = wgt.shape[0]
    assert q_col % tn == 0 and k_col % tn == 0 and N - k_col == tn and (v_col - k_col) % dh == 0
    kern = functools.partial(_in_proj_kernel, q_blk=q_col // tn, k_blk=k_col // tn,
                             k_cols=v_col - k_col, dh=dh, norm_rows=min(tm, 256))
    return pl.pallas_call(
        kern,
        grid=(T // tm, N // tn),
        in_specs=[
            pl.BlockSpec((tm, D), lambda i, j: (i, 0)),
            pl.BlockSpec((1, D), lambda i, j: (0, 0)),
            pl.BlockSpec((D, tn), lambda i, j: (0, j)),
            pl.BlockSpec((ng, D), lambda i, j: (0, 0)),
            pl.BlockSpec((1, dh), lambda i, j: (0, 0)),
            pl.BlockSpec((1, dh), lambda i, j: (0, 0)),
        ],
        out_specs=[
            pl.BlockSpec((tm, tn), lambda i, j: (i, j)),
            pl.BlockSpec((ng, tm), lambda i, j: (0, i)),
        ],
        out_shape=[
            jax.ShapeDtypeStruct((T, N), BF16),
            jax.ShapeDtypeStruct((ng, T), F32),
        ],
        scratch_shapes=[pltpu.VMEM((tm, D), BF16)],
        compiler_params=_params(("parallel", "arbitrary")),
        name="in_proj",
    )(x2, g, w, wgt, qg, kg)


def _mlstm_kernel(q_ref, k_ref, v_ref, gt_ref, bcol_ref, h_ref,
                  c_ref, n_ref, m_ref, *, nh, scale):
    d = pl.program_id(0)
    hh = pl.program_id(2)
    c = pl.program_id(3)
    L = q_ref.shape[0]

    @pl.when(c == 0)
    def _():
        c_ref[...] = jnp.zeros_like(c_ref)
        n_ref[...] = jnp.zeros_like(n_ref)
        m_ref[...] = jnp.zeros_like(m_ref)

    col_i = d * 2 * nh + hh
    col_f = col_i + nh
    fwd = d == 0

    r_idx = lax.broadcasted_iota(I32, (L, L), 0)
    s_idx = lax.broadcasted_iota(I32, (L, L), 1)
    seen = (r_idx - s_idx) * (1 - 2 * d) >= 0
    tri = seen.astype(F32)

    grow = gt_ref[...] + bcol_ref[...]
    lf_rows = _log_sigmoid(grow)
    b_rows = lax.dot_general(lf_rows, tri, (((1,), (1,)), ((), ())),
                             precision=HIGHEST, preferred_element_type=F32)
    sub = lax.broadcasted_iota(I32, grow.shape, 0)
    pick_row = lambda a, ridx: jnp.sum(jnp.where(sub == ridx, a, 0.0), axis=0, keepdims=True)
    li_row = pick_row(grow, col_i)
    b_row = pick_row(b_rows, col_f)
    cols = jnp.concatenate([li_row, b_row, jnp.zeros((LANES - 2, L), F32)], axis=0).T
    li_col = cols[:, 0:1]
    b_col = cols[:, 1:2]

    m_prev = m_ref[...]
    dmat = jnp.where(seen, b_col - b_row + li_row, -jnp.inf)
    inter = b_col + m_prev
    m_row = jnp.maximum(inter, jnp.max(dmat, axis=1, keepdims=True))
    w_inter = jnp.exp(inter - m_row)

    q = q_ref[...]
    k = k_ref[...]
    v = v_ref[...]
    qk = lax.dot_general(q, k, (((1,), (1,)), ((), ())), preferred_element_type=F32)
    s = qk * (jnp.exp(dmat - m_row) * scale)
    qf = q.astype(F32)
    num = (w_inter * jnp.dot(q, c_ref[...].astype(BF16), preferred_element_type=F32)
           + jnp.dot(s.astype(BF16), v, preferred_element_type=F32))
    den = (w_inter * jnp.sum(qf * n_ref[...], axis=1, keepdims=True)
           + jnp.sum(s, axis=1, keepdims=True))
    h_ref[...] = num / jnp.maximum(jnp.abs(den), jnp.exp(-m_row))

    b_last = jnp.max(jnp.where(fwd, b_col[L - 1:L, :], b_col[0:1, :]), axis=0, keepdims=True)
    g_col = b_last - b_col + li_col
    m_new = jnp.maximum(b_last + m_prev, jnp.max(g_col, axis=0, keepdims=True))
    decay = jnp.exp(b_last + m_prev - m_new)
    wk = jnp.exp(g_col - m_new) * scale
    kw = k.astype(F32) * wk
    c_ref[...] = decay * c_ref[...] + lax.dot_general(
        kw.astype(BF16), v, (((0,), (0,)), ((), ())), preferred_element_type=F32)
    n_ref[...] = decay * n_ref[...] + jnp.sum(kw, axis=0, keepdims=True)
    m_ref[...] = m_new


def mlstm(proj, gates_t, bias_col, *, B, S, nh, dh, L):
    nc = S // L
    T = B * S
    ng = gates_t.shape[0]

    def row(d, b, c):
        return b * nc + c + d * (nc - 1 - 2 * c)

    kern = functools.partial(_mlstm_kernel, nh=nh, scale=dh ** -0.5)
    return pl.pallas_call(
        kern,
        grid=(2, B, nh, nc),
        in_specs=[
            pl.BlockSpec((L, dh), lambda d, b, h, c: (row(d, b, c), h)),
            pl.BlockSpec((L, dh), lambda d, b, h, c: (row(d, b, c), nh + h)),
            pl.BlockSpec((L, dh), lambda d, b, h, c: (row(d, b, c), 2 * nh + h)),
            pl.BlockSpec((ng, L), lambda d, b, h, c: (0, row(d, b, c))),
            pl.BlockSpec((ng, 1), lambda d, b, h, c: (0, 0)),
        ],
        out_specs=pl.BlockSpec((None, L, dh), lambda d, b, h, c: (d, row(d, b, c), h)),
        out_shape=jax.ShapeDtypeStruct((2, T, nh * dh), F32),
        scratch_shapes=[pltpu.VMEM((dh, dh), F32), pltpu.VMEM((1, dh), F32), pltpu.VMEM((1, 1), F32)],
        compiler_params=_params(("parallel", "parallel", "parallel", "arbitrary")),
        name="mlstm",
    )(proj, proj, proj, gates_t, bias_col)


def _attn_kernel(sink_ref, q_ref, k0_ref, k1_ref, k2_ref, v0_ref, v1_ref, v2_ref,
                 bias_ref, o_ref, *, group, dh):
    g = pl.program_id(1)
    i = pl.program_id(2)
    nb = pl.num_programs(2)
    blk = q_ref.shape[0]

    kcat = jnp.concatenate([k0_ref[...], k1_ref[...], k2_ref[...]], axis=0)
    vcat = jnp.concatenate([v0_ref[...], v1_ref[...], v2_ref[...]], axis=0)
    lane = lax.broadcasted_iota(I32, (1, 3 * blk), 1)
    ok = ((lane >= blk) | (i > 0)) & ((lane < 2 * blk) | (i < nb - 1))
    for hq in range(group):
        s = lax.dot_general(q_ref[:, hq * dh:(hq + 1) * dh], kcat, (((1,), (1,)), ((), ())),
                            preferred_element_type=F32)
        s = jnp.where(ok, s + bias_ref[hq], -jnp.inf)
        sink = sink_ref[g * group + hq]
        m = jnp.maximum(jnp.max(s, axis=1, keepdims=True), sink)
        p = jnp.exp(s - m)
        den = jnp.sum(p, axis=1, keepdims=True) + jnp.exp(sink - m)
        o = jnp.dot(p.astype(BF16), vcat, preferred_element_type=F32) / den
        o_ref[:, hq * dh:(hq + 1) * dh] = o.astype(o_ref.dtype)


def attention(proj, sink, bias_tab, *, B, S, q_col, k_col, v_col, nkv, group, dh, blk):
    nb = S // blk
    T = B * S
    gw = group * dh
    assert q_col % group == 0

    def kv_spec(col0, off):
        def imap(b, g, i, sink_ref):
            j = jnp.clip(i + off, 0, nb - 1)
            return (b * nb + j, col0 + g)
        return pl.BlockSpec((blk, dh), imap)

    kern = functools.partial(_attn_kernel, group=group, dh=dh)
    grid_spec = pltpu.PrefetchScalarGridSpec(
        num_scalar_prefetch=1,
        grid=(B, nkv, nb),
        in_specs=[
            pl.BlockSpec((blk, gw), lambda b, g, i, s: (b * nb + i, q_col // group + g)),
            kv_spec(k_col, -1), kv_spec(k_col, 0), kv_spec(k_col, 1),
            kv_spec(v_col, -1), kv_spec(v_col, 0), kv_spec(v_col, 1),
            pl.BlockSpec((group, blk, 3 * blk), lambda b, g, i, s: (g, 0, 0)),
        ],
        out_specs=pl.BlockSpec((blk, gw), lambda b, g, i, s: (b * nb + i, g)),
    )
    return pl.pallas_call(
        kern,
        grid_spec=grid_spec,
        out_shape=jax.ShapeDtypeStruct((T, nkv * gw), BF16),
        compiler_params=_params(("parallel", "parallel", "arbitrary")),
        name="window_attn",
    )(sink, proj, proj, proj, proj, proj, proj, proj, bias_tab)


def _t5_bucket(rel):
    half = REL_BUCKETS // 2
    max_exact = half // 2
    ret = jnp.where(rel > 0, half, 0)
    n = jnp.abs(rel)
    nf = jnp.maximum(n, 1).astype(F32)
    large = max_exact + (jnp.log(nf / max_exact) / math.log(REL_MAX_DIST / max_exact)
                         * (half - max_exact)).astype(I32)
    large = jnp.minimum(large, half - 1)
    return ret + jnp.where(n < max_exact, n, large)


def attn_bias_table(rel_bias, blk, window):
    q_local = jnp.arange(blk)
    k_local = jnp.arange(3 * blk) - blk
    rel = k_local[None, :] - q_local[:, None]
    offs = jnp.arange(-(blk - 1), 3 * blk) - blk
    hit = _t5_bucket(offs)[:, None] == jnp.arange(REL_BUCKETS)
    by_off = jnp.sum(jnp.where(hit[None], rel_bias.astype(F32).T[:, None, :], 0.0), axis=-1)
    bias = jnp.stack([by_off[:, blk - 1 - q: blk - 1 - q + 3 * blk] for q in range(blk)], axis=1)
    return jnp.where((jnp.abs(rel) <= window)[None], bias, -jnp.inf)


def _merge_kernel(hs_ref, mo_ref, gm_ref, ga_ref, ha_ref, x_ref, ng_ref, n2_ref,
                  wm_ref, wa_ref, wo_ref, x1_ref, xn2_ref, xn2b_ref, *, nh):
    hsum = hs_ref[0] + hs_ref[1]
    dh = hsum.shape[1] // nh
    segs = []
    for hh in range(nh):
        seg = hsum[:, hh * dh:(hh + 1) * dh]
        segs.append(seg * lax.rsqrt(jnp.mean(seg * seg, axis=-1, keepdims=True) + EPS))
    hm = jnp.concatenate(segs, axis=1) * ng_ref[...] * jax.nn.sigmoid(mo_ref[...].astype(F32))
    br_m = jnp.dot(hm.astype(BF16), wm_ref[...], preferred_element_type=F32)
    br_a = jnp.dot(ha_ref[...], wa_ref[...], preferred_element_type=F32)
    merged = (jax.nn.sigmoid(gm_ref[...].astype(F32)) * br_m
              + jax.nn.sigmoid(ga_ref[...].astype(F32)) * br_a)
    x1 = x_ref[...] + jnp.dot(merged.astype(BF16), wo_ref[...], preferred_element_type=F32)
    x1_ref[...] = x1
    xn2 = x1 * lax.rsqrt(jnp.mean(x1 * x1, axis=-1, keepdims=True) + EPS) * n2_ref[...]
    xn2_ref[...] = xn2
    xn2b_ref[...] = xn2.astype(BF16)


def merge(hs, proj, ha, x2, ng, n2, wm, wa, wo, *, nh, mo_col, gm_col, ga_col, tm):
    T, D = x2.shape
    W = hs.shape[2]
    const = lambda shape: pl.BlockSpec(shape, lambda i: (0,) * len(shape), pipeline_mode=pl.Buffered(1))
    kern = functools.partial(_merge_kernel, nh=nh)
    return pl.pallas_call(
        kern,
        grid=(T // tm,),
        in_specs=[
            pl.BlockSpec((2, tm, W), lambda i: (0, i, 0)),
            pl.BlockSpec((tm, W), lambda i: (i, mo_col)),
            pl.BlockSpec((tm, D), lambda i: (i, gm_col)),
            pl.BlockSpec((tm, D), lambda i: (i, ga_col)),
            pl.BlockSpec((tm, ha.shape[1]), lambda i: (i, 0)),
            pl.BlockSpec((tm, D), lambda i: (i, 0)),
            const((1, W)), const((1, D)),
            const(wm.shape), const(wa.shape), const(wo.shape),
        ],
        out_specs=[pl.BlockSpec((tm, D), lambda i: (i, 0))] * 3,
        out_shape=[jax.ShapeDtypeStruct((T, D), F32), jax.ShapeDtypeStruct((T, D), F32),
                   jax.ShapeDtypeStruct((T, D), BF16)],
        compiler_params=_params(("parallel",)),
        name="merge",
    )(hs, proj, proj, proj, ha, x2, ng, n2, wm, wa, wo)


def _extract_top(vals, n_take, emit):
    rows = vals.shape[0]
    ridx = lax.broadcasted_iota(I32, vals.shape, 0)

    def body(kk, cur):
        mx = jnp.max(cur, axis=0, keepdims=True)
        idx = jnp.min(jnp.where(cur == mx, ridx, rows), axis=0, keepdims=True)
        hit = ridx == idx
        emit(kk, mx, idx, hit)
        return jnp.where(hit, -jnp.inf, cur)

    lax.fori_loop(0, n_take, body, vals)


def _route_kernel(pq_ref, keys_ref, e_ref, gate_ref, sv_ref, si_ref, pe_ref, *, nkeys, topk):
    tm = pq_ref.shape[0]
    qh = pq_ref.shape[1] // 2
    for p in range(2):
        sub = lax.dot_general(keys_ref[p], pq_ref[:, p * qh:(p + 1) * qh],
                              (((1,), (1,)), ((), ())), preferred_element_type=F32)

        def emit(kk, mx, idx, hit, p=p):
            sv_ref[pl.ds(kk, 1), p * tm:(p + 1) * tm] = mx
            si_ref[pl.ds(kk, 1), p * tm:(p + 1) * tm] = idx

        _extract_top(sub, topk, emit)

    sv0, sv1 = sv_ref[:, :tm], sv_ref[:, tm:]
    si0, si1 = si_ref[:, :tm], si_ref[:, tm:]
    width = [topk // (a + 1) for a in range(topk)]
    pad = pe_ref.shape[0] - sum(width)
    cand = jnp.concatenate([sv0[a:a + 1, :] + sv1[:width[a], :] for a in range(topk)]
                           + [jnp.full((pad, tm), -jnp.inf, F32)], axis=0)
    pe_ref[...] = jnp.concatenate([si0[a:a + 1, :] * nkeys + si1[:width[a], :] for a in range(topk)]
                                  + [jnp.full((pad, tm), -1, I32)], axis=0)

    def emit2(kk, mx, idx, hit):
        gate_ref[pl.ds(kk, 1), :] = mx
        e_ref[pl.ds(kk, 1), :] = jnp.max(jnp.where(hit, pe_ref[...], -1), axis=0, keepdims=True)

    _extract_top(cand, topk, emit2)
    cv = gate_ref[...]
    ex = jnp.exp(cv - cv[0:1, :])
    gate_ref[...] = ex / jnp.sum(ex, axis=0, keepdims=True)


def _matmul_kernel(a_ref, w_ref, o_ref):
    o_ref[...] = jnp.dot(a_ref[...], w_ref[...], preferred_element_type=F32).astype(o_ref.dtype)


def matmul(a, w, *, out_dtype, tm, tn):
    T, K = a.shape
    N = w.shape[1]
    return pl.pallas_call(
        _matmul_kernel,
        grid=(T // tm, N // tn),
        in_specs=[pl.BlockSpec((tm, K), lambda i, j: (i, 0)), pl.BlockSpec((K, tn), lambda i, j: (0, j))],
        out_specs=pl.BlockSpec((tm, tn), lambda i, j: (i, j)),
        out_shape=jax.ShapeDtypeStruct((T, N), out_dtype),
        compiler_params=_params(("parallel", "parallel")),
        name="peer_query",
    )(a, w)


def peer_route(pq, keys, *, topk, tm):
    T = pq.shape[0]
    nheads, _, nkeys, kd = keys.shape
    kern = functools.partial(_route_kernel, nkeys=nkeys, topk=topk)
    ncand = sum(topk // (a + 1) for a in range(topk))
    ncand = -(-ncand // SUBLANES) * SUBLANES
    return pl.pallas_call(
        kern,
        grid=(T // tm, nheads),
        in_specs=[
            pl.BlockSpec((tm, 2 * kd), lambda i, h: (i, h)),
            pl.BlockSpec((None, 2, nkeys, kd), lambda i, h: (h, 0, 0, 0)),
        ],
        out_specs=[pl.BlockSpec((topk, tm), lambda i, h: (h, i)),
                   pl.BlockSpec((topk, tm), lambda i, h: (h, i))],
        out_shape=[jax.ShapeDtypeStruct((nheads * topk, T), I32),
                   jax.ShapeDtypeStruct((nheads * topk, T), F32)],
        scratch_shapes=[pltpu.VMEM((topk, 2 * tm), F32), pltpu.VMEM((topk, 2 * tm), I32),
                        pltpu.VMEM((ncand, tm), I32)],
        compiler_params=_params(("parallel", "arbitrary")),
        name="peer_route",
    )(pq, keys)


def _gelu(s):
    return 0.5 * s * (1.0 + lax.erf(s * (2.0 ** -0.5)))


def _expert_kernel(idx_cur_ref, idx_nxt_ref, tab_ref, x_ref, xn_ref, gate_ref, o_ref,
                   gbuf_a, gbuf_b, act_ref, sem_a, sem_b, *, nsel, tb, n_first):
    i = pl.program_id(0)
    nstep = pl.num_programs(0)
    rows = tb * nsel
    ngrp = nsel // SUBLANES
    nchunk = gbuf_a.shape[0]
    hi_mask = jnp.int32(-65536)
    gates = gate_ref[...]
    glane = lax.broadcasted_iota(I32, gates.shape, 1)
    tok0 = (i * 2 * tb) % gates.shape[1]

    def issue(idx_ref, idx_base, t, dst, sems, k_lo, k_hi):
        tok_idx = idx_ref.at[pl.ds(idx_base + t * nsel, nsel)]
        for kk in range(k_lo, k_hi):
            pltpu.make_async_copy(tab_ref.at[tok_idx[kk]], dst.at[:, t * nsel + kk, :],
                                  sems.at[t]).start(priority=kk % 2)

    def wait_token(buf, other, sems, t):
        rows_t = pl.ds(t * nsel, nsel)
        pltpu.make_async_copy(other.at[:, rows_t, :], buf.at[:, rows_t, :], sems.at[t]).wait()

    @pl.when(i == 0)
    def _():
        def body(t, carry):
            issue(idx_cur_ref, 0, t, gbuf_a, sem_a, 0, nsel)
            return carry
        lax.fori_loop(0, tb, body, 0)

    def phase(src, src_sems, dst, dst_sems, idx_ref, idx_base, tok_off):
        def score(t, acts):
            wait_token(src, dst, src_sems, t)
            issue(idx_ref, idx_base, t, dst, dst_sems, 0, n_first)
            xt = xn_ref[tok_off + t]
            xb = [jnp.broadcast_to(xt[c:c + 1, :], (SUBLANES, LANES)) for c in range(nchunk)]
            cols = []
            for gidx in range(ngrp):
                r0 = pl.multiple_of(t * nsel + gidx * SUBLANES, SUBLANES)
                acc = [jnp.zeros((SUBLANES, LANES), F32) for _ in range(2)]
                for c in range(nchunk):
                    uf = lax.bitcast_convert_type(src[c, pl.ds(r0, SUBLANES), :] << 16, F32)
                    acc[c % 2] = acc[c % 2] + uf * xb[c]
                cols.append(jnp.sum(acc[0] + acc[1], axis=1, keepdims=True))
            s = jnp.concatenate(cols, axis=0)
            return jnp.where(glane == tok0 + tok_off + t, s, acts)

        scores = lax.fori_loop(0, tb, score, jnp.zeros(gates.shape, F32))
        act_ref[...] = _gelu(scores) * gates

        def mix(t, carry):
            issue(idx_ref, idx_base, t, dst, dst_sems, n_first, nsel)
            a = jnp.sum(jnp.where(glane == tok0 + tok_off + t, act_ref[...], 0.0), axis=1, keepdims=True)
            accs = [jnp.zeros((SUBLANES, LANES), F32) for _ in range(nchunk)]
            for gidx in range(ngrp):
                r0 = pl.multiple_of(t * nsel + gidx * SUBLANES, SUBLANES)
                a_g = jnp.broadcast_to(a[gidx * SUBLANES:(gidx + 1) * SUBLANES, :], (SUBLANES, LANES))
                for c in range(nchunk):
                    vf = lax.bitcast_convert_type(src[c, pl.ds(r0, SUBLANES), :] & hi_mask, F32)
                    accs[c] = accs[c] + vf * a_g
            out = jnp.concatenate([jnp.sum(acc, axis=0, keepdims=True) for acc in accs], axis=0)
            o_ref[tok_off + t] = x_ref[tok_off + t] + out
            return carry

        lax.fori_loop(0, tb, mix, 0)

    phase(gbuf_a, sem_a, gbuf_b, sem_b, idx_cur_ref, rows, 0)
    phase(gbuf_b, sem_b, gbuf_a, sem_a, idx_nxt_ref, 0, tb)

    @pl.when(i == nstep - 1)
    def _():
        def body(t, carry):
            wait_token(gbuf_a, gbuf_b, sem_a, t)
            return carry
        lax.fori_loop(0, tb, body, 0)


def peer_experts(idx_flat, tab, x1_3, xn2_3, gate_t, *, nsel, tb):
    T, C, _ = x1_3.shape
    nstep = T // (2 * tb)
    rows = tb * nsel
    gblk = min(T, LANES)
    assert gblk % (2 * tb) == 0
    kern = functools.partial(_expert_kernel, nsel=nsel, tb=tb, n_first=(5 * nsel) // 8)
    return pl.pallas_call(
        kern,
        grid=(nstep,),
        in_specs=[
            pl.BlockSpec((2 * rows,), lambda i: (i,), memory_space=pltpu.SMEM),
            pl.BlockSpec((2 * rows,), lambda i: (jnp.minimum(i + 1, nstep - 1),), memory_space=pltpu.SMEM),
            pl.BlockSpec(memory_space=pl.ANY),
            pl.BlockSpec((2 * tb, C, LANES), lambda i: (i, 0, 0)),
            pl.BlockSpec((2 * tb, C, LANES), lambda i: (i, 0, 0)),
            pl.BlockSpec((nsel, gblk), lambda i: (0, (i * 2 * tb) // gblk)),
        ],
        out_specs=pl.BlockSpec((2 * tb, C, LANES), lambda i: (i, 0, 0)),
        out_shape=jax.ShapeDtypeStruct((T, C, LANES), F32),
        scratch_shapes=[pltpu.VMEM((C, rows, LANES), I32), pltpu.VMEM((C, rows, LANES), I32),
                        pltpu.VMEM((nsel, gblk), F32),
                        pltpu.SemaphoreType.DMA((tb,)), pltpu.SemaphoreType.DMA((tb,))],
        compiler_params=_params(("arbitrary",)),
        name="peer_experts",
    )(idx_flat, idx_flat, tab, x1_3, xn2_3, gate_t)


def _pack_kernel(u_ref, v_ref, o_ref):
    bu = lax.bitcast_convert_type(u_ref[...].astype(BF16).astype(F32), I32)
    bv = lax.bitcast_convert_type(v_ref[...].astype(BF16).astype(F32), I32)
    word = bv | lax.shift_right_logical(bu, 16)
    for c in range(o_ref.shape[1]):
        o_ref[:, c, :] = word[:, c * LANES:(c + 1) * LANES]


def pack_expert_table(peer_u, peer_v, *, te=256):
    E, D = peer_u.shape
    return pl.pallas_call(
        _pack_kernel,
        grid=(E // te,),
        in_specs=[pl.BlockSpec((te, D), lambda i: (i, 0)), pl.BlockSpec((te, D), lambda i: (i, 0))],
        out_specs=pl.BlockSpec((te, D // LANES, LANES), lambda i: (i, 0, 0)),
        out_shape=jax.ShapeDtypeStruct((E, D // LANES, LANES), I32),
        compiler_params=_params(("parallel",)),
        name="pack_experts",
    )(peer_u, peer_v)


def kernel(x, norm1_g, w_in, mlstm_gate_b, mlstm_norm_g, w_m_proj, attn_q_norm_g, attn_k_norm_g,
           attn_sink, rel_bias, w_a_proj, w_out, norm2_g, peer_wq, peer_keys, peer_u, peer_v):
    B, S, D = x.shape
    T = B * S
    mw = w_m_proj.shape[0]
    dh_m = mw // M_HEADS
    aw = w_a_proj.shape[0]
    kvw = A_KV_HEADS * A_DH
    ngate = 4 * M_HEADS

    splits = (mw, mw, mw, mw, ngate, aw, kvw, kvw, D, D)
    offs = [0]
    for sz in splits:
        offs.append(offs[-1] + sz)
    w_bf = w_in.astype(BF16)
    seg = lambda n: w_bf[:, offs[n]:offs[n + 1]]
    w_rest = jnp.concatenate([w_bf[:, :offs[4]], seg(8), seg(9), seg(5), seg(6), seg(7)], axis=1)
    wgt = w_in[:, offs[4]:offs[5]].T

    x2 = x.reshape(T, D)
    q_col = 4 * mw + 2 * D
    proj, gates_t = in_proj(x2, norm1_g.reshape(1, D), w_rest, wgt,
                            attn_q_norm_g.reshape(1, A_DH).astype(F32), attn_k_norm_g.reshape(1, A_DH).astype(F32),
                            tm=1024, tn=1024, q_col=q_col, k_col=q_col + aw, v_col=q_col + aw + kvw, dh=A_DH)

    gb = mlstm_gate_b.reshape(ngate, 1).astype(F32)
    hs = mlstm(proj, gates_t, gb, B=B, S=S, nh=M_HEADS, dh=dh_m, L=256)

    a_col0 = q_col // A_DH
    ha = attention(proj, attn_sink.astype(F32), attn_bias_table(rel_bias, WINDOW, WINDOW),
                   B=B, S=S, q_col=a_col0, k_col=a_col0 + A_HEADS, v_col=a_col0 + A_HEADS + A_KV_HEADS,
                   nkv=A_KV_HEADS, group=A_GROUP, dh=A_DH, blk=WINDOW)

    x1, xn2, xn2b = merge(hs, proj, ha, x2, mlstm_norm_g.reshape(1, mw), norm2_g.reshape(1, D),
                    w_m_proj.astype(BF16), w_a_proj.astype(BF16), w_out.astype(BF16),
                    nh=M_HEADS, mo_col=3, gm_col=(4 * mw) // D, ga_col=(4 * mw) // D + 1, tm=128)

    pq = matmul(xn2b, peer_wq.astype(BF16), out_dtype=BF16, tm=512, tn=1024)
    e_t, g_t = peer_route(pq, peer_keys.astype(BF16), topk=PEER_TOPK, tm=256)
    nsel = PEER_HEADS * PEER_TOPK
    idx_flat = e_t.T.reshape(T * nsel)
    C = D // LANES
    y = peer_experts(idx_flat, pack_expert_table(peer_u, peer_v), x1.reshape(T, C, LANES),
                     xn2.reshape(T, C, LANES), g_t, nsel=nsel, tb=16)
    return y.reshape(B, S, D)
```

```python
import functools
import math

import jax
import jax.numpy as jnp
from jax import lax
from jax.experimental import pallas as pl
from jax.experimental.pallas import tpu as pltpu

F32 = jnp.float32
BF16 = jnp.bfloat16
I32 = jnp.int32
HIGHEST = lax.Precision.HIGHEST
EPS = 1e-6
LANES = 128
SUBLANES = 8
VMEM_LIMIT = 56 * 1024 * 1024

M_HEADS = 4
A_HEADS = 16
A_KV_HEADS = 4
A_GROUP = A_HEADS // A_KV_HEADS
A_DH = 128
WINDOW = 128
REL_BUCKETS = 32
REL_MAX_DIST = 128
PEER_HEADS = 8
PEER_NKEYS = 128
PEER_TOPK = 16


def _params(sem, vmem=VMEM_LIMIT):
    return pltpu.CompilerParams(dimension_semantics=sem, vmem_limit_bytes=vmem)


def _log_sigmoid(x):
    return jnp.minimum(x, 0.0) - jnp.log1p(jnp.exp(-jnp.abs(x)))


def _head_rms(t, gain, dh):
    parts = []
    for h in range(t.shape[1] // dh):
        seg = t[:, h * dh:(h + 1) * dh]
        parts.append(seg * lax.rsqrt(jnp.mean(seg * seg, axis=-1, keepdims=True) + EPS) * gain)
    return jnp.concatenate(parts, axis=1)


def _in_proj_kernel(x_ref, g_ref, w_ref, wgt_ref, qg_ref, kg_ref, o_ref, ogt_ref, xn_ref,
                    *, q_blk, k_blk, k_cols, dh, norm_rows):
    j = pl.program_id(1)

    @pl.when(j == 0)
    def _():
        for r0 in range(0, x_ref.shape[0], norm_rows):
            rows = slice(r0, r0 + norm_rows)
            x = x_ref[rows, :]
            y = x * lax.rsqrt(jnp.mean(x * x, axis=-1, keepdims=True) + EPS) * g_ref[...]
            xn_ref[rows, :] = y.astype(BF16)
            ogt_ref[:, rows] = lax.dot_general(wgt_ref[...], y, (((1,), (1,)), ((), ())),
                                               precision=HIGHEST, preferred_element_type=F32)

    acc = jnp.dot(xn_ref[...], w_ref[...], preferred_element_type=F32)

    @pl.when(j < q_blk)
    def _():
        o_ref[...] = acc.astype(o_ref.dtype)

    @pl.when((j >= q_blk) & (j < k_blk))
    def _():
        o_ref[...] = _head_rms(acc, qg_ref[...] * (dh ** -0.5), dh).astype(o_ref.dtype)

    @pl.when(j == k_blk)
    def _():
        o_ref[...] = jnp.concatenate([_head_rms(acc[:, :k_cols], kg_ref[...], dh), acc[:, k_cols:]],
                                     axis=1).astype(o_ref.dtype)


def in_proj(x2, g, w, wgt, qg, kg, *, tm, tn, q_col, k_col, v_col, dh):
    T, D = x2.shape
    N = w.shape[1]
    ng = wgt.shape[0]
    assert q_col % tn == 0 and k_col % tn == 0 and N - k_col == tn and (v_col - k_col) % dh == 0
    kern = functools.partial(_in_proj_kernel, q_blk=q_col // tn, k_blk=k_col // tn,
                             k_cols=v_col - k_col, dh=dh, norm_rows=min(tm, 256))
    return pl.pallas_call(
        kern,
        grid=(T // tm, N // tn),
        in_specs=[
            pl.BlockSpec((tm, D), lambda i, j: (i, 0)),
            pl.BlockSpec((1, D), lambda i, j: (0, 0)),
            pl.BlockSpec((D, tn), lambda i, j: (0, j)),
            pl.BlockSpec((ng, D), lambda i, j: (0, 0)),
            pl.BlockSpec((1, dh), lambda i, j: (0, 0)),
            pl.BlockSpec((1, dh), lambda i, j: (0, 0)),
        ],
        out_specs=[
            pl.BlockSpec((tm, tn), lambda i, j: (i, j)),
            pl.BlockSpec((ng, tm), lambda i, j: (0, i)),
        ],
        out_shape=[
            jax.ShapeDtypeStruct((T, N), BF16),
            jax.ShapeDtypeStruct((ng, T), F32),
        ],
        scratch_shapes=[pltpu.VMEM((tm, D), BF16)],
        compiler_params=_params(("parallel", "arbitrary")),
        name="in_proj",
    )(x2, g, w, wgt, qg, kg)


def _mlstm_kernel(q_ref, k_ref, v_ref, gt_ref, bcol_ref, h_ref,
                  c_ref, n_ref, m_ref, *, nh, scale):
    d = pl.program_id(0)
    hh = pl.program_id(2)
    c = pl.program_id(3)
    L = q_ref.shape[0]

    @pl.when(c == 0)
    def _():
        c_ref[...] = jnp.zeros_like(c_ref)
        n_ref[...] = jnp.zeros_like(n_ref)
        m_ref[...] = jnp.zeros_like(m_ref)

    col_i = d * 2 * nh + hh
    col_f = col_i + nh
    fwd = d == 0

    r_idx = lax.broadcasted_iota(I32, (L, L), 0)
    s_idx = lax.broadcasted_iota(I32, (L, L), 1)
    seen = (r_idx - s_idx) * (1 - 2 * d) >= 0
    tri = seen.astype(F32)

    grow = gt_ref[...] + bcol_ref[...]
    lf_rows = _log_sigmoid(grow)
    b_rows = lax.dot_general(lf_rows, tri, (((1,), (1,)), ((), ())),
                             precision=HIGHEST, preferred_element_type=F32)
    sub = lax.broadcasted_iota(I32, grow.shape, 0)
    pick_row = lambda a, ridx: jnp.sum(jnp.where(sub == ridx, a, 0.0), axis=0, keepdims=True)
    li_row = pick_row(grow, col_i)
    b_row = pick_row(b_rows, col_f)
    cols = jnp.concatenate([li_row, b_row, jnp.zeros((LANES - 2, L), F32)], axis=0).T
    li_col = cols[:, 0:1]
    b_col = cols[:, 1:2]

    m_prev = m_ref[...]
    dmat = jnp.where(seen, b_col - b_row + li_row, -jnp.inf)
    inter = b_col + m_prev
    m_row = jnp.maximum(inter, jnp.max(dmat, axis=1, keepdims=True))
    w_inter = jnp.exp(inter - m_row)

    q = q_ref[...]
    k = k_ref[...]
    v = v_ref[...]
    qk = lax.dot_general(q, k, (((1,), (1,)), ((), ())), preferred_element_type=F32)
    s = qk * (jnp.exp(dmat - m_row) * scale)
    qf = q.astype(F32)
    num = (w_inter * jnp.dot(q, c_ref[...].astype(BF16), preferred_element_type=F32)
           + jnp.dot(s.astype(BF16), v, preferred_element_type=F32))
    den = (w_inter * jnp.sum(qf * n_ref[...], axis=1, keepdims=True)
           + jnp.sum(s, axis=1, keepdims=True))
    h_ref[...] = num / jnp.maximum(jnp.abs(den), jnp.exp(-m_row))

    b_last = jnp.max(jnp.where(fwd, b_col[L - 1:L, :], b_col[0:1, :]), axis=0, keepdims=True)
    g_col = b_last - b_col + li_col
    m_new = jnp.maximum(b_last + m_prev, jnp.max(g_col, axis=0, keepdims=True))
    decay = jnp.exp(b_last + m_prev - m_new)
    wk = jnp.exp(g_col - m_new) * scale
    kw = k.astype(F32) * wk
    c_ref[...] = decay * c_ref[...] + lax.dot_general(
        kw.astype(BF16), v, (((0,), (0,)), ((), ())), preferred_element_type=F32)
    n_ref[...] = decay * n_ref[...] + jnp.sum(kw, axis=0, keepdims=True)
    m_ref[...] = m_new


def mlstm(proj, gates_t, bias_col, *, B, S, nh, dh, L):
    nc = S // L
    T = B * S
    ng = gates_t.shape[0]

    def row(d, b, c):
        return b * nc + c + d * (nc - 1 - 2 * c)

    kern = functools.partial(_mlstm_kernel, nh=nh, scale=dh ** -0.5)
    return pl.pallas_call(
        kern,
        grid=(2, B, nh, nc),
        in_specs=[
            pl.BlockSpec((L, dh), lambda d, b, h, c: (row(d, b, c), h)),
            pl.BlockSpec((L, dh), lambda d, b, h, c: (row(d, b, c), nh + h)),
            pl.BlockSpec((L, dh), lambda d, b, h, c: (row(d, b, c), 2 * nh + h)),
            pl.BlockSpec((ng, L), lambda d, b, h, c: (0, row(d, b, c))),
            pl.BlockSpec((ng, 1), lambda d, b, h, c: (0, 0)),
        ],
        out_specs=pl.BlockSpec((None, L, dh), lambda d, b, h, c: (d, row(d, b, c), h)),
        out_shape=jax.ShapeDtypeStruct((2, T, nh * dh), F32),
        scratch_shapes=[pltpu.VMEM((dh, dh), F32), pltpu.VMEM((1, dh), F32), pltpu.VMEM((1, 1), F32)],
        compiler_params=_params(("parallel", "parallel", "parallel", "arbitrary")),
        name="mlstm",
    )(proj, proj, proj, gates_t, bias_col)


def _attn_kernel(sink_ref, q_ref, k0_ref, k1_ref, k2_ref, v0_ref, v1_ref, v2_ref,
                 bias_ref, o_ref, *, group, dh):
    g = pl.program_id(1)
    i = pl.program_id(2)
    nb = pl.num_programs(2)
    blk = q_ref.shape[0]

    kcat = jnp.concatenate([k0_ref[...], k1_ref[...], k2_ref[...]], axis=0)
    vcat = jnp.concatenate([v0_ref[...], v1_ref[...], v2_ref[...]], axis=0)
    lane = lax.broadcasted_iota(I32, (1, 3 * blk), 1)
    ok = ((lane >= blk) | (i > 0)) & ((lane < 2 * blk) | (i < nb - 1))
    for hq in range(group):
        s = lax.dot_general(q_ref[:, hq * dh:(hq + 1) * dh], kcat, (((1,), (1,)), ((), ())),
                            preferred_element_type=F32)
        s = jnp.where(ok, s + bias_ref[hq], -jnp.inf)
        sink = sink_ref[g * group + hq]
        m = jnp.maximum(jnp.max(s, axis=1, keepdims=True), sink)
        p = jnp.exp(s - m)
        den = jnp.sum(p, axis=1, keepdims=True) + jnp.exp(sink - m)
        o = jnp.dot(p.astype(BF16), vcat, preferred_element_type=F32) / den
        o_ref[:, hq * dh:(hq + 1) * dh] = o.astype(o_ref.dtype)


def attention(proj, sink, bias_tab, *, B, S, q_col, k_col, v_col, nkv, group, dh, blk):
    nb = S // blk
    T = B * S
    gw = group * dh
    assert q_col % group == 0

    def kv_spec(col0, off):
        def imap(b, g, i, sink_ref):
            j = jnp.clip(i + off, 0, nb - 1)
            return (b * nb + j, col0 + g)
        return pl.BlockSpec((blk, dh), imap)

    kern = functools.partial(_attn_kernel, group=group, dh=dh)
    grid_spec = pltpu.PrefetchScalarGridSpec(
        num_scalar_prefetch=1,
        grid=(B, nkv, nb),
        in_specs=[
            pl.BlockSpec((blk, gw), lambda b, g, i, s: (b * nb + i, q_col // group + g)),
            kv_spec(k_col, -1), kv_spec(k_col, 0), kv_spec(k_col, 1),
            kv_spec(v_col, -1), kv_spec(v_col, 0), kv_spec(v_col, 1),
            pl.BlockSpec((group, blk, 3 * blk), lambda b, g, i, s: (g, 0, 0)),
        ],
        out_specs=pl.BlockSpec((blk, gw), lambda b, g, i, s: (b * nb + i, g)),
    )
    return pl.pallas_call(
        kern,
        grid_spec=grid_spec,
        out_shape=jax.ShapeDtypeStruct((T, nkv * gw), BF16),
        compiler_params=_params(("parallel", "parallel", "arbitrary")),
        name="window_attn",
    )(sink, proj, proj, proj, proj, proj, proj, proj, bias_tab)


def _t5_bucket(rel):
    half = REL_BUCKETS // 2
    max_exact = half // 2
    ret = jnp.where(rel > 0, half, 0)
    n = jnp.abs(rel)
    nf = jnp.maximum(n, 1).astype(F32)
    large = max_exact + (jnp.log(nf / max_exact) / math.log(REL_MAX_DIST / max_exact)
                         * (half - max_exact)).astype(I32)
    large = jnp.minimum(large, half - 1)
    return ret + jnp.where(n < max_exact, n, large)


def attn_bias_table(rel_bias, blk, window):
    q_local = jnp.arange(blk)
    k_local = jnp.arange(3 * blk) - blk
    rel = k_local[None, :] - q_local[:, None]
    offs = jnp.arange(-(blk - 1), 3 * blk) - blk
    hit = _t5_bucket(offs)[:, None] == jnp.arange(REL_BUCKETS)
    by_off = jnp.sum(jnp.where(hit[None], rel_bias.astype(F32).T[:, None, :], 0.0), axis=-1)
    bias = jnp.stack([by_off[:, blk - 1 - q: blk - 1 - q + 3 * blk] for q in range(blk)], axis=1)
    return jnp.where((jnp.abs(rel) <= window)[None], bias, -jnp.inf)


def _merge_kernel(hs_ref, mo_ref, gm_ref, ga_ref, ha_ref, x_ref, ng_ref, n2_ref,
                  wm_ref, wa_ref, wo_ref, x1_ref, xn2_ref, xn2b_ref, *, nh):
    hsum = hs_ref[0] + hs_ref[1]
    dh = hsum.shape[1] // nh
    segs = []
    for hh in range(nh):
        seg = hsum[:, hh * dh:(hh + 1) * dh]
        segs.append(seg * lax.rsqrt(jnp.mean(seg * seg, axis=-1, keepdims=True) + EPS))
    hm = jnp.concatenate(segs, axis=1) * ng_ref[...] * jax.nn.sigmoid(mo_ref[...].astype(F32))
    br_m = jnp.dot(hm.astype(BF16), wm_ref[...], preferred_element_type=F32)
    br_a = jnp.dot(ha_ref[...], wa_ref[...], preferred_element_type=F32)
    merged = (jax.nn.sigmoid(gm_ref[...].astype(F32)) * br_m
              + jax.nn.sigmoid(ga_ref[...].astype(F32)) * br_a)
    x1 = x_ref[...] + jnp.dot(merged.astype(BF16), wo_ref[...], preferred_element_type=F32)
    x1_ref[...] = x1
    xn2 = x1 * lax.rsqrt(jnp.mean(x1 * x1, axis=-1, keepdims=True) + EPS) * n2_ref[...]
    xn2_ref[...] = xn2
    xn2b_ref[...] = xn2.astype(BF16)


def merge(hs, proj, ha, x2, ng, n2, wm, wa, wo, *, nh, mo_col, gm_col, ga_col, tm):
    T, D = x2.shape
    W = hs.shape[2]
    const = lambda shape: pl.BlockSpec(shape, lambda i: (0,) * len(shape), pipeline_mode=pl.Buffered(1))
    kern = functools.partial(_merge_kernel, nh=nh)
    return pl.pallas_call(
        kern,
        grid=(T // tm,),
        in_specs=[
            pl.BlockSpec((2, tm, W), lambda i: (0, i, 0)),
            pl.BlockSpec((tm, W), lambda i: (i, mo_col)),
            pl.BlockSpec((tm, D), lambda i: (i, gm_col)),
            pl.BlockSpec((tm, D), lambda i: (i, ga_col)),
            pl.BlockSpec((tm, ha.shape[1]), lambda i: (i, 0)),
            pl.BlockSpec((tm, D), lambda i: (i, 0)),
            const((1, W)), const((1, D)),
            const(wm.shape), const(wa.shape), const(wo.shape),
        ],
        out_specs=[pl.BlockSpec((tm, D), lambda i: (i, 0))] * 3,
        out_shape=[jax.ShapeDtypeStruct((T, D), F32), jax.ShapeDtypeStruct((T, D), F32),
                   jax.ShapeDtypeStruct((T, D), BF16)],
        compiler_params=_params(("parallel",)),
        name="merge",
    )(hs, proj, proj, proj, ha, x2, ng, n2, wm, wa, wo)


def _extract_top(vals, n_take, emit):
    rows = vals.shape[0]
    ridx = lax.broadcasted_iota(I32, vals.shape, 0)

    def body(kk, cur):
        mx = jnp.max(cur, axis=0, keepdims=True)
        idx = jnp.min(jnp.where(cur == mx, ridx, rows), axis=0, keepdims=True)
        hit = ridx == idx
        emit(kk, mx, idx, hit)
        return jnp.where(hit, -jnp.inf, cur)

    lax.fori_loop(0, n_take, body, vals)


def _route_kernel(pq_ref, keys_ref, e_ref, gate_ref, sv_ref, si_ref, pe_ref, *, nkeys, topk):
    tm = pq_ref.shape[0]
    qh = pq_ref.shape[1] // 2
    for p in range(2):
        sub = lax.dot_general(keys_ref[p], pq_ref[:, p * qh:(p + 1) * qh],
                              (((1,), (1,)), ((), ())), preferred_element_type=F32)

        def emit(kk, mx, idx, hit, p=p):
            sv_ref[pl.ds(kk, 1), p * tm:(p + 1) * tm] = mx
            si_ref[pl.ds(kk, 1), p * tm:(p + 1) * tm] = idx

        _extract_top(sub, topk, emit)

    sv0, sv1 = sv_ref[:, :tm], sv_ref[:, tm:]
    si0, si1 = si_ref[:, :tm], si_ref[:, tm:]
    width = [topk // (a + 1) for a in range(topk)]
    pad = pe_ref.shape[0] - sum(width)
    cand = jnp.concatenate([sv0[a:a + 1, :] + sv1[:width[a], :] for a in range(topk)]
                           + [jnp.full((pad, tm), -jnp.inf, F32)], axis=0)
    pe_ref[...] = jnp.concatenate([si0[a:a + 1, :] * nkeys + si1[:width[a], :] for a in range(topk)]
                                  + [jnp.full((pad, tm), -1, I32)], axis=0)

    def emit2(kk, mx, idx, hit):
        gate_ref[pl.ds(kk, 1), :] = mx
        e_ref[pl.ds(kk, 1), :] = jnp.max(jnp.where(hit, pe_ref[...], -1), axis=0, keepdims=True)

    _extract_top(cand, topk, emit2)
    cv = gate_ref[...]
    ex = jnp.exp(cv - cv[0:1, :])
    gate_ref[...] = ex / jnp.sum(ex, axis=0, keepdims=True)


def _matmul_kernel(a_ref, w_ref, o_ref):
    o_ref[...] = jnp.dot(a_ref[...], w_ref[...], preferred_element_type=F32).astype(o_ref.dtype)


def matmul(a, w, *, out_dtype, tm, tn):
    T, K = a.shape
    N = w.shape[1]
    return pl.pallas_call(
        _matmul_kernel,
        grid=(T // tm, N // tn),
        in_specs=[pl.BlockSpec((tm, K), lambda i, j: (i, 0)), pl.BlockSpec((K, tn), lambda i, j: (0, j))],
        out_specs=pl.BlockSpec((tm, tn), lambda i, j: (i, j)),
        out_shape=jax.ShapeDtypeStruct((T, N), out_dtype),
        compiler_params=_params(("parallel", "parallel")),
        name="peer_query",
    )(a, w)


def peer_route(pq, keys, *, topk, tm):
    T = pq.shape[0]
    nheads, _, nkeys, kd = keys.shape
    kern = functools.partial(_route_kernel, nkeys=nkeys, topk=topk)
    ncand = sum(topk // (a + 1) for a in range(topk))
    ncand = -(-ncand // SUBLANES) * SUBLANES
    return pl.pallas_call(
        kern,
        grid=(T // tm, nheads),
        in_specs=[
            pl.BlockSpec((tm, 2 * kd), lambda i, h: (i, h)),
            pl.BlockSpec((None, 2, nkeys, kd), lambda i, h: (h, 0, 0, 0)),
        ],
        out_specs=[pl.BlockSpec((topk, tm), lambda i, h: (h, i)),
                   pl.BlockSpec((topk, tm), lambda i, h: (h, i))],
        out_shape=[jax.ShapeDtypeStruct((nheads * topk, T), I32),
                   jax.ShapeDtypeStruct((nheads * topk, T), F32)],
        scratch_shapes=[pltpu.VMEM((topk, 2 * tm), F32), pltpu.VMEM((topk, 2 * tm), I32),
                        pltpu.VMEM((ncand, tm), I32)],
        compiler_params=_params(("parallel", "arbitrary")),
        name="peer_route",
    )(pq, keys)


def _gelu(s):
    return 0.5 * s * (1.0 + lax.erf(s * (2.0 ** -0.5)))


def _expert_kernel(idx_cur_ref, idx_nxt_ref, tab_ref, x_ref, xn_ref, gate_ref, o_ref,
                   gbuf_a, gbuf_b, act_ref, sem_a, sem_b, *, nsel, tb, n_first):
    i = pl.program_id(0)
    nstep = pl.num_programs(0)
    rows = tb * nsel
    ngrp = nsel // SUBLANES
    nchunk = gbuf_a.shape[0]
    hi_mask = jnp.int32(-65536)
    gates = gate_ref[...]
    glane = lax.broadcasted_iota(I32, gates.shape, 1)
    tok0 = (i * 2 * tb) % gates.shape[1]

    def issue(idx_ref, idx_base, t, dst, sems, k_lo, k_hi):
        tok_idx = idx_ref.at[pl.ds(idx_base + t * nsel, nsel)]
        for kk in range(k_lo, k_hi):
            pltpu.make_async_copy(tab_ref.at[tok_idx[kk]], dst.at[:, t * nsel + kk, :],
                                  sems.at[t]).start(priority=kk % 2)

    def wait_token(buf, other, sems, t):
        rows_t = pl.ds(t * nsel, nsel)
        pltpu.make_async_copy(other.at[:, rows_t, :], buf.at[:, rows_t, :], sems.at[t]).wait()

    @pl.when(i == 0)
    def _():
        def body(t, carry):
            issue(idx_cur_ref, 0, t, gbuf_a, sem_a, 0, nsel)
            return carry
        lax.fori_loop(0, tb, body, 0)

    def phase(src, src_sems, dst, dst_sems, idx_ref, idx_base, tok_off):
        def score(t, acts):
            wait_token(src, dst, src_sems, t)
            issue(idx_ref, idx_base, t, dst, dst_sems, 0, n_first)
            xt = xn_ref[tok_off + t]
            xb = [jnp.broadcast_to(xt[c:c + 1, :], (SUBLANES, LANES)) for c in range(nchunk)]
            cols = []
            for gidx in range(ngrp):
                r0 = pl.multiple_of(t * nsel + gidx * SUBLANES, SUBLANES)
                acc = [jnp.zeros((SUBLANES, LANES), F32) for _ in range(2)]
                for c in range(nchunk):
                    uf = lax.bitcast_convert_type(src[c, pl.ds(r0, SUBLANES), :] << 16, F32)
                    acc[c % 2] = acc[c % 2] + uf * xb[c]
                cols.append(jnp.sum(acc[0] + acc[1], axis=1, keepdims=True))
            s = jnp.concatenate(cols, axis=0)
            return jnp.where(glane == tok0 + tok_off + t, s, acts)

        scores = lax.fori_loop(0, tb, score, jnp.zeros(gates.shape, F32))
        act_ref[...] = _gelu(scores) * gates

        def mix(t, carry):
            issue(idx_ref, idx_base, t, dst, dst_sems, n_first, nsel)
            a = jnp.sum(jnp.where(glane == tok0 + tok_off + t, act_ref[...], 0.0), axis=1, keepdims=True)
            accs = [jnp.zeros((SUBLANES, LANES), F32) for _ in range(nchunk)]
            for gidx in range(ngrp):
                r0 = pl.multiple_of(t * nsel + gidx * SUBLANES, SUBLANES)
                a_g = jnp.broadcast_to(a[gidx * SUBLANES:(gidx + 1) * SUBLANES, :], (SUBLANES, LANES))
                for c in range(nchunk):
                    vf = lax.bitcast_convert_type(src[c, pl.ds(r0, SUBLANES), :] & hi_mask, F32)
                    accs[c] = accs[c] + vf * a_g
            out = jnp.concatenate([jnp.sum(acc, axis=0, keepdims=True) for acc in accs], axis=0)
            o_ref[tok_off + t] = x_ref[tok_off + t] + out
            return carry

        lax.fori_loop(0, tb, mix, 0)

    phase(gbuf_a, sem_a, gbuf_b, sem_b, idx_cur_ref, rows, 0)
    phase(gbuf_b, sem_b, gbuf_a, sem_a, idx_nxt_ref, 0, tb)

    @pl.when(i == nstep - 1)
    def _():
        def body(t, carry):
            wait_token(gbuf_a, gbuf_b, sem_a, t)
            return carry
        lax.fori_loop(0, tb, body, 0)


def peer_experts(idx_flat, tab, x1_3, xn2_3, gate_t, *, nsel, tb):
    T, C, _ = x1_3.shape
    nstep = T // (2 * tb)
    rows = tb * nsel
    gblk = min(T, LANES)
    assert gblk % (2 * tb) == 0
    kern = functools.partial(_expert_kernel, nsel=nsel, tb=tb, n_first=(5 * nsel) // 8)
    return pl.pallas_call(
        kern,
        grid=(nstep,),
        in_specs=[
            pl.BlockSpec((2 * rows,), lambda i: (i,), memory_space=pltpu.SMEM),
            pl.BlockSpec((2 * rows,), lambda i: (jnp.minimum(i + 1, nstep - 1),), memory_space=pltpu.SMEM),
            pl.BlockSpec(memory_space=pl.ANY),
            pl.BlockSpec((2 * tb, C, LANES), lambda i: (i, 0, 0)),
            pl.BlockSpec((2 * tb, C, LANES), lambda i: (i, 0, 0)),
            pl.BlockSpec((nsel, gblk), lambda i: (0, (i * 2 * tb) // gblk)),
        ],
        out_specs=pl.BlockSpec((2 * tb, C, LANES), lambda i: (i, 0, 0)),
        out_shape=jax.ShapeDtypeStruct((T, C, LANES), F32),
        scratch_shapes=[pltpu.VMEM((C, rows, LANES), I32), pltpu.VMEM((C, rows, LANES), I32),
                        pltpu.VMEM((nsel, gblk), F32),
                        pltpu.SemaphoreType.DMA((tb,)), pltpu.SemaphoreType.DMA((tb,))],
        compiler_params=_params(("arbitrary",)),
        name="peer_experts",
    )(idx_flat, idx_flat, tab, x1_3, xn2_3, gate_t)


def _pack_kernel(u_ref, v_ref, o_ref):
    bu = lax.bitcast_convert_type(u_ref[...].astype(BF16).astype(F32), I32)
    bv = lax.bitcast_convert_type(v_ref[...].astype(BF16).astype(F32), I32)
    o_ref[...] = bv | lax.shift_right_logical(bu, 16)


def pack_expert_table(peer_u, peer_v, *, te=512):
    E, D = peer_u.shape
    return pl.pallas_call(
        _pack_kernel,
        grid=(E // te,),
        in_specs=[pl.BlockSpec((te, D), lambda i: (i, 0)), pl.BlockSpec((te, D), lambda i: (i, 0))],
        out_specs=pl.BlockSpec((te, D), lambda i: (i, 0)),
        out_shape=jax.ShapeDtypeStruct((E, D), I32),
        compiler_params=_params(("parallel",)),
        name="pack_experts",
    )(peer_u, peer_v)


def kernel(x, norm1_g, w_in, mlstm_gate_b, mlstm_norm_g, w_m_proj, attn_q_norm_g, attn_k_norm_g,
           attn_sink, rel_bias, w_a_proj, w_out, norm2_g, peer_wq, peer_keys, peer_u, peer_v):
    B, S, D = x.shape
    T = B * S
    mw = w_m_proj.shape[0]
    dh_m = mw // M_HEADS
    aw = w_a_proj.shape[0]
    kvw = A_KV_HEADS * A_DH
    ngate = 4 * M_HEADS

    splits = (mw, mw, mw, mw, ngate, aw, kvw, kvw, D, D)
    offs = [0]
    for sz in splits:
        offs.append(offs[-1] + sz)
    w_bf = w_in.astype(BF16)
    seg = lambda n: w_bf[:, offs[n]:offs[n + 1]]
    w_rest = jnp.concatenate([w_bf[:, :offs[4]], seg(8), seg(9), seg(5), seg(6), seg(7)], axis=1)
    wgt = w_in[:, offs[4]:offs[5]].T

    x2 = x.reshape(T, D)
    q_col = 4 * mw + 2 * D
    proj, gates_t = in_proj(x2, norm1_g.reshape(1, D), w_rest, wgt,
                            attn_q_norm_g.reshape(1, A_DH).astype(F32), attn_k_norm_g.reshape(1, A_DH).astype(F32),
                            tm=1024, tn=1024, q_col=q_col, k_col=q_col + aw, v_col=q_col + aw + kvw, dh=A_DH)

    gb = mlstm_gate_b.reshape(ngate, 1).astype(F32)
    hs = mlstm(proj, gates_t, gb, B=B, S=S, nh=M_HEADS, dh=dh_m, L=512)

    a_col0 = q_col // A_DH
    ha = attention(proj, attn_sink.astype(F32), attn_bias_table(rel_bias, WINDOW, WINDOW),
                   B=B, S=S, q_col=a_col0, k_col=a_col0 + A_HEADS, v_col=a_col0 + A_HEADS + A_KV_HEADS,
                   nkv=A_KV_HEADS, group=A_GROUP, dh=A_DH, blk=WINDOW)

    x1, xn2, xn2b = merge(hs, proj, ha, x2, mlstm_norm_g.reshape(1, mw), norm2_g.reshape(1, D),
                    w_m_proj.astype(BF16), w_a_proj.astype(BF16), w_out.astype(BF16),
                    nh=M_HEADS, mo_col=3, gm_col=(4 * mw) // D, ga_col=(4 * mw) // D + 1, tm=128)

    pq = matmul(xn2b, peer_wq.astype(BF16), out_dtype=BF16, tm=512, tn=1024)
    e_t, g_t = peer_route(pq, peer_keys.astype(BF16), topk=PEER_TOPK, tm=256)
    nsel = PEER_HEADS * PEER_TOPK
    idx_flat = e_t.T.reshape(T * nsel)
    C = D // LANES
    table = pack_expert_table(peer_u, peer_v).reshape(peer_u.shape[0], C, LANES)
    y = peer_experts(idx_flat, table, x1.reshape(T, C, LANES),
                     xn2.reshape(T, C, LANES), g_t, nsel=nsel, tb=16)
    return y.reshape(B, S, D)
```

```python
import functools
import math

import jax
import jax.numpy as jnp
from jax import lax
from jax.experimental import pallas as pl
from jax.experimental.pallas import tpu as pltpu

F32 = jnp.float32
BF16 = jnp.bfloat16
I32 = jnp.int32
HIGHEST = lax.Precision.HIGHEST
EPS = 1e-6
LANES = 128
SUBLANES = 8
VMEM_LIMIT = 56 * 1024 * 1024

M_HEADS = 4
A_HEADS = 16
A_KV_HEADS = 4
A_GROUP = A_HEADS // A_KV_HEADS
A_DH = 128
WINDOW = 128
REL_BUCKETS = 32
REL_MAX_DIST = 128
PEER_HEADS = 8
PEER_NKEYS = 128
PEER_TOPK = 16


def _params(sem, vmem=VMEM_LIMIT):
    return pltpu.CompilerParams(dimension_semantics=sem, vmem_limit_bytes=vmem)


def _log_sigmoid(x):
    return jnp.minimum(x, 0.0) - jnp.log1p(jnp.exp(-jnp.abs(x)))


def _head_rms(t, gain, dh):
    parts = []
    for h in range(t.shape[1] // dh):
        seg = t[:, h * dh:(h + 1) * dh]
        parts.append(seg * lax.rsqrt(jnp.mean(seg * seg, axis=-1, keepdims=True) + EPS) * gain)
    return jnp.concatenate(parts, axis=1)


def _in_proj_kernel(x_ref, g_ref, w_ref, wgt_ref, qg_ref, kg_ref, o_ref, ogt_ref, xn_ref,
                    *, q_blk, k_blk, k_cols, dh, norm_rows):
    j = pl.program_id(1)

    @pl.when(j == 0)
    def _():
        for r0 in range(0, x_ref.shape[0], norm_rows):
            rows = slice(r0, r0 + norm_rows)
            x = x_ref[rows, :]
            y = x * lax.rsqrt(jnp.mean(x * x, axis=-1, keepdims=True) + EPS) * g_ref[...]
            xn_ref[rows, :] = y.astype(BF16)
            ogt_ref[:, rows] = lax.dot_general(wgt_ref[...], y, (((1,), (1,)), ((), ())),
                                               precision=HIGHEST, preferred_element_type=F32)

    acc = jnp.dot(xn_ref[...], w_ref[...], preferred_element_type=F32)

    @pl.when(j < q_blk)
    def _():
        o_ref[...] = acc.astype(o_ref.dtype)

    @pl.when((j >= q_blk) & (j < k_blk))
    def _():
        o_ref[...] = _head_rms(acc, qg_ref[...] * (dh ** -0.5), dh).astype(o_ref.dtype)

    @pl.when(j == k_blk)
    def _():
        o_ref[...] = jnp.concatenate([_head_rms(acc[:, :k_cols], kg_ref[...], dh), acc[:, k_cols:]],
                                     axis=1).astype(o_ref.dtype)


def in_proj(x2, g, w, wgt, qg, kg, *, tm, tn, q_col, k_col, v_col, dh):
    T, D = x2.shape
    N = w.shape[1]
    ng = wgt.shape[0]
    assert q_col % tn == 0 and k_col % tn == 0 and N - k_col == tn and (v_col - k_col) % dh == 0
    kern = functools.partial(_in_proj_kernel, q_blk=q_col // tn, k_blk=k_col // tn,
                             k_cols=v_col - k_col, dh=dh, norm_rows=min(tm, 256))
    return pl.pallas_call(
        kern,
        grid=(T // tm, N // tn),
        in_specs=[
            pl.BlockSpec((tm, D), lambda i, j: (i, 0)),
            pl.BlockSpec((1, D), lambda i, j: (0, 0)),
            pl.BlockSpec((D, tn), lambda i, j: (0, j)),
            pl.BlockSpec((ng, D), lambda i, j: (0, 0)),
            pl.BlockSpec((1, dh), lambda i, j: (0, 0)),
            pl.BlockSpec((1, dh), lambda i, j: (0, 0)),
        ],
        out_specs=[
            pl.BlockSpec((tm, tn), lambda i, j: (i, j)),
            pl.BlockSpec((ng, tm), lambda i, j: (0, i)),
        ],
        out_shape=[
            jax.ShapeDtypeStruct((T, N), BF16),
            jax.ShapeDtypeStruct((ng, T), F32),
        ],
        scratch_shapes=[pltpu.VMEM((tm, D), BF16)],
        compiler_params=_params(("parallel", "arbitrary")),
        name="in_proj",
    )(x2, g, w, wgt, qg, kg)


def _mlstm_kernel(q_ref, k_ref, v_ref, gt_ref, bcol_ref, h_ref,
                  c_ref, n_ref, m_ref, *, nh, scale):
    d = pl.program_id(0)
    hh = pl.program_id(2)
    c = pl.program_id(3)
    L = q_ref.shape[0]

    @pl.when(c == 0)
    def _():
        c_ref[...] = jnp.zeros_like(c_ref)
        n_ref[...] = jnp.zeros_like(n_ref)
        m_ref[...] = jnp.zeros_like(m_ref)

    col_i = d * 2 * nh + hh
    col_f = col_i + nh
    fwd = d == 0

    r_idx = lax.broadcasted_iota(I32, (L, L), 0)
    s_idx = lax.broadcasted_iota(I32, (L, L), 1)
    seen = (r_idx - s_idx) * (1 - 2 * d) >= 0
    tri = seen.astype(F32)

    grow = gt_ref[...] + bcol_ref[...]
    lf_rows = _log_sigmoid(grow)
    b_rows = lax.dot_general(lf_rows, tri, (((1,), (1,)), ((), ())),
                             precision=HIGHEST, preferred_element_type=F32)
    sub = lax.broadcasted_iota(I32, grow.shape, 0)
    pick_row = lambda a, ridx: jnp.sum(jnp.where(sub == ridx, a, 0.0), axis=0, keepdims=True)
    li_row = pick_row(grow, col_i)
    b_row = pick_row(b_rows, col_f)
    cols = jnp.concatenate([li_row, b_row, jnp.zeros((LANES - 2, L), F32)], axis=0).T
    li_col = cols[:, 0:1]
    b_col = cols[:, 1:2]

    m_prev = m_ref[...]
    dmat = jnp.where(seen, b_col - b_row + li_row, -jnp.inf)
    inter = b_col + m_prev
    m_row = jnp.maximum(inter, jnp.max(dmat, axis=1, keepdims=True))
    w_inter = jnp.exp(inter - m_row)

    q = q_ref[...]
    k = k_ref[...]
    v = v_ref[...]
    qk = lax.dot_general(q, k, (((1,), (1,)), ((), ())), preferred_element_type=F32)
    s = qk * (jnp.exp(dmat - m_row) * scale)
    qf = q.astype(F32)
    num = (w_inter * jnp.dot(q, c_ref[...].astype(BF16), preferred_element_type=F32)
           + jnp.dot(s.astype(BF16), v, preferred_element_type=F32))
    den = (w_inter * jnp.sum(qf * n_ref[...], axis=1, keepdims=True)
           + jnp.sum(s, axis=1, keepdims=True))
    h_ref[...] = num / jnp.maximum(jnp.abs(den), jnp.exp(-m_row))

    b_last = jnp.max(jnp.where(fwd, b_col[L - 1:L, :], b_col[0:1, :]), axis=0, keepdims=True)
    g_col = b_last - b_col + li_col
    m_new = jnp.maximum(b_last + m_prev, jnp.max(g_col, axis=0, keepdims=True))
    decay = jnp.exp(b_last + m_prev - m_new)
    wk = jnp.exp(g_col - m_new) * scale
    kw = k.astype(F32) * wk
    c_ref[...] = decay * c_ref[...] + lax.dot_general(
        kw.astype(BF16), v, (((0,), (0,)), ((), ())), preferred_element_type=F32)
    n_ref[...] = decay * n_ref[...] + jnp.sum(kw, axis=0, keepdims=True)
    m_ref[...] = m_new


def mlstm(proj, gates_t, bias_col, *, B, S, nh, dh, L):
    nc = S // L
    T = B * S
    ng = gates_t.shape[0]

    def row(d, b, c):
        return b * nc + c + d * (nc - 1 - 2 * c)

    kern = functools.partial(_mlstm_kernel, nh=nh, scale=dh ** -0.5)
    return pl.pallas_call(
        kern,
        grid=(2, B, nh, nc),
        in_specs=[
            pl.BlockSpec((L, dh), lambda d, b, h, c: (row(d, b, c), h)),
            pl.BlockSpec((L, dh), lambda d, b, h, c: (row(d, b, c), nh + h)),
            pl.BlockSpec((L, dh), lambda d, b, h, c: (row(d, b, c), 2 * nh + h)),
            pl.BlockSpec((ng, L), lambda d, b, h, c: (0, row(d, b, c))),
            pl.BlockSpec((ng, 1), lambda d, b, h, c: (0, 0)),
        ],
        out_specs=pl.BlockSpec((None, L, dh), lambda d, b, h, c: (d, row(d, b, c), h)),
        out_shape=jax.ShapeDtypeStruct((2, T, nh * dh), F32),
        scratch_shapes=[pltpu.VMEM((dh, dh), F32), pltpu.VMEM((1, dh), F32), pltpu.VMEM((1, 1), F32)],
        compiler_params=_params(("parallel", "parallel", "parallel", "arbitrary")),
        name="mlstm",
    )(proj, proj, proj, gates_t, bias_col)


def _attn_kernel(sink_ref, q_ref, k0_ref, k1_ref, k2_ref, v0_ref, v1_ref, v2_ref,
                 bias_ref, o_ref, *, group, dh):
    i = pl.program_id(1)
    nb = pl.num_programs(1)
    blk = q_ref.shape[0]
    nkv = k0_ref.shape[1] // dh

    lane = lax.broadcasted_iota(I32, (1, 3 * blk), 1)
    ok = ((lane >= blk) | (i > 0)) & ((lane < 2 * blk) | (i < nb - 1))
    for g in range(nkv):
        cols = slice(g * dh, (g + 1) * dh)
        kcat = jnp.concatenate([k0_ref[:, cols], k1_ref[:, cols], k2_ref[:, cols]], axis=0)
        vcat = jnp.concatenate([v0_ref[:, cols], v1_ref[:, cols], v2_ref[:, cols]], axis=0)
        for hq in range(g * group, (g + 1) * group):
            s = lax.dot_general(q_ref[:, hq * dh:(hq + 1) * dh], kcat, (((1,), (1,)), ((), ())),
                                preferred_element_type=F32)
            s = jnp.where(ok, s + bias_ref[hq], -jnp.inf)
            sink = sink_ref[hq]
            m = jnp.maximum(jnp.max(s, axis=1, keepdims=True), sink)
            p = jnp.exp(s - m)
            den = jnp.sum(p, axis=1, keepdims=True) + jnp.exp(sink - m)
            o = jnp.dot(p.astype(BF16), vcat, preferred_element_type=F32) / den
            o_ref[:, hq * dh:(hq + 1) * dh] = o.astype(o_ref.dtype)


def attention(proj, sink, bias_tab, *, B, S, q_col, k_col, v_col, nkv, group, dh, blk):
    nb = S // blk
    T = B * S
    qw = nkv * group * dh
    kvw = nkv * dh
    assert (q_col * dh) % qw == 0 and (k_col * dh) % kvw == 0 and (v_col * dh) % kvw == 0

    def kv_spec(col0, off):
        def imap(b, i, sink_ref):
            j = jnp.clip(i + off, 0, nb - 1)
            return (b * nb + j, (col0 * dh) // kvw)
        return pl.BlockSpec((blk, kvw), imap)

    kern = functools.partial(_attn_kernel, group=group, dh=dh)
    grid_spec = pltpu.PrefetchScalarGridSpec(
        num_scalar_prefetch=1,
        grid=(B, nb),
        in_specs=[
            pl.BlockSpec((blk, qw), lambda b, i, s: (b * nb + i, (q_col * dh) // qw)),
            kv_spec(k_col, -1), kv_spec(k_col, 0), kv_spec(k_col, 1),
            kv_spec(v_col, -1), kv_spec(v_col, 0), kv_spec(v_col, 1),
            pl.BlockSpec((nkv * group, blk, 3 * blk), lambda b, i, s: (0, 0, 0)),
        ],
        out_specs=pl.BlockSpec((blk, qw), lambda b, i, s: (b * nb + i, 0)),
    )
    return pl.pallas_call(
        kern,
        grid_spec=grid_spec,
        out_shape=jax.ShapeDtypeStruct((T, qw), BF16),
        compiler_params=_params(("parallel", "arbitrary")),
        name="window_attn",
    )(sink, proj, proj, proj, proj, proj, proj, proj, bias_tab)


def _t5_bucket(rel):
    half = REL_BUCKETS // 2
    max_exact = half // 2
    ret = jnp.where(rel > 0, half, 0)
    n = jnp.abs(rel)
    nf = jnp.maximum(n, 1).astype(F32)
    large = max_exact + (jnp.log(nf / max_exact) / math.log(REL_MAX_DIST / max_exact)
                         * (half - max_exact)).astype(I32)
    large = jnp.minimum(large, half - 1)
    return ret + jnp.where(n < max_exact, n, large)


def attn_bias_table(rel_bias, blk, window):
    q_local = jnp.arange(blk)
    k_local = jnp.arange(3 * blk) - blk
    rel = k_local[None, :] - q_local[:, None]
    offs = jnp.arange(-(blk - 1), 3 * blk) - blk
    hit = _t5_bucket(offs)[:, None] == jnp.arange(REL_BUCKETS)
    by_off = jnp.sum(jnp.where(hit[None], rel_bias.astype(F32).T[:, None, :], 0.0), axis=-1)
    bias = jnp.stack([by_off[:, blk - 1 - q: blk - 1 - q + 3 * blk] for q in range(blk)], axis=1)
    return jnp.where((jnp.abs(rel) <= window)[None], bias, -jnp.inf)


def _merge_kernel(hs_ref, mo_ref, gm_ref, ga_ref, ha_ref, x_ref, ng_ref, n2_ref,
                  wm_ref, wa_ref, wo_ref, x1_ref, xn2_ref, xn2b_ref, *, nh):
    hsum = hs_ref[0] + hs_ref[1]
    dh = hsum.shape[1] // nh
    segs = []
    for hh in range(nh):
        seg = hsum[:, hh * dh:(hh + 1) * dh]
        segs.append(seg * lax.rsqrt(jnp.mean(seg * seg, axis=-1, keepdims=True) + EPS))
    hm = jnp.concatenate(segs, axis=1) * ng_ref[...] * jax.nn.sigmoid(mo_ref[...].astype(F32))
    br_m = jnp.dot(hm.astype(BF16), wm_ref[...], preferred_element_type=F32)
    br_a = jnp.dot(ha_ref[...], wa_ref[...], preferred_element_type=F32)
    merged = (jax.nn.sigmoid(gm_ref[...].astype(F32)) * br_m
              + jax.nn.sigmoid(ga_ref[...].astype(F32)) * br_a)
    x1 = x_ref[...] + jnp.dot(merged.astype(BF16), wo_ref[...], preferred_element_type=F32)
    x1_ref[...] = x1
    xn2 = x1 * lax.rsqrt(jnp.mean(x1 * x1, axis=-1, keepdims=True) + EPS) * n2_ref[...]
    xn2_ref[...] = xn2
    xn2b_ref[...] = xn2.astype(BF16)


def merge(hs, proj, ha, x2, ng, n2, wm, wa, wo, *, nh, mo_col, gm_col, ga_col, tm):
    T, D = x2.shape
    W = hs.shape[2]
    const = lambda shape: pl.BlockSpec(shape, lambda i: (0,) * len(shape), pipeline_mode=pl.Buffered(1))
    kern = functools.partial(_merge_kernel, nh=nh)
    return pl.pallas_call(
        kern,
        grid=(T // tm,),
        in_specs=[
            pl.BlockSpec((2, tm, W), lambda i: (0, i, 0)),
            pl.BlockSpec((tm, W), lambda i: (i, mo_col)),
            pl.BlockSpec((tm, D), lambda i: (i, gm_col)),
            pl.BlockSpec((tm, D), lambda i: (i, ga_col)),
            pl.BlockSpec((tm, ha.shape[1]), lambda i: (i, 0)),
            pl.BlockSpec((tm, D), lambda i: (i, 0)),
            const((1, W)), const((1, D)),
            const(wm.shape), const(wa.shape), const(wo.shape),
        ],
        out_specs=[pl.BlockSpec((tm, D), lambda i: (i, 0))] * 3,
        out_shape=[jax.ShapeDtypeStruct((T, D), F32), jax.ShapeDtypeStruct((T, D), F32),
                   jax.ShapeDtypeStruct((T, D), BF16)],
        compiler_params=_params(("parallel",)),
        name="merge",
    )(hs, proj, proj, proj, ha, x2, ng, n2, wm, wa, wo)


def _extract_top(vals, n_take, emit):
    rows = vals.shape[0]
    ridx = lax.broadcasted_iota(I32, vals.shape, 0)

    def body(kk, cur):
        mx = jnp.max(cur, axis=0, keepdims=True)
        idx = jnp.min(jnp.where(cur == mx, ridx, rows), axis=0, keepdims=True)
        hit = ridx == idx
        emit(kk, mx, idx, hit)
        return jnp.where(hit, -jnp.inf, cur)

    lax.fori_loop(0, n_take, body, vals)


def _route_kernel(pq_ref, keys_ref, e_ref, gate_ref, sv_ref, si_ref, pe_ref, *, nkeys, topk):
    tm = pq_ref.shape[0]
    nheads = keys_ref.shape[0]
    qh = pq_ref.shape[1] // (2 * nheads)
    width = [topk // (a + 1) for a in range(topk)]
    pad = pe_ref.shape[0] - sum(width)
    for h in range(nheads):
        for p in range(2):
            c0 = (2 * h + p) * qh
            sub = lax.dot_general(keys_ref[h, p], pq_ref[:, c0:c0 + qh],
                                  (((1,), (1,)), ((), ())), preferred_element_type=F32)

            def emit(kk, mx, idx, hit, p=p):
                sv_ref[pl.ds(kk, 1), p * tm:(p + 1) * tm] = mx
                si_ref[pl.ds(kk, 1), p * tm:(p + 1) * tm] = idx

            _extract_top(sub, topk, emit)

        sv0, sv1 = sv_ref[:, :tm], sv_ref[:, tm:]
        si0, si1 = si_ref[:, :tm], si_ref[:, tm:]
        cand = jnp.concatenate([sv0[a:a + 1, :] + sv1[:width[a], :] for a in range(topk)]
                               + [jnp.full((pad, tm), -jnp.inf, F32)], axis=0)
        pe_ref[...] = jnp.concatenate([si0[a:a + 1, :] * nkeys + si1[:width[a], :] for a in range(topk)]
                                      + [jnp.full((pad, tm), -1, I32)], axis=0)
        r0 = h * topk

        def emit2(kk, mx, idx, hit, r0=r0):
            gate_ref[pl.ds(r0 + kk, 1), :] = mx
            e_ref[pl.ds(r0 + kk, 1), :] = jnp.max(jnp.where(hit, pe_ref[...], -1), axis=0, keepdims=True)

        _extract_top(cand, topk, emit2)
        cv = gate_ref[r0:r0 + topk, :]
        ex = jnp.exp(cv - cv[0:1, :])
        gate_ref[r0:r0 + topk, :] = ex / jnp.sum(ex, axis=0, keepdims=True)


def _matmul_kernel(a_ref, w_ref, o_ref):
    o_ref[...] = jnp.dot(a_ref[...], w_ref[...], preferred_element_type=F32).astype(o_ref.dtype)


def matmul(a, w, *, out_dtype, tm, tn):
    T, K = a.shape
    N = w.shape[1]
    return pl.pallas_call(
        _matmul_kernel,
        grid=(T // tm, N // tn),
        in_specs=[pl.BlockSpec((tm, K), lambda i, j: (i, 0)), pl.BlockSpec((K, tn), lambda i, j: (0, j))],
        out_specs=pl.BlockSpec((tm, tn), lambda i, j: (i, j)),
        out_shape=jax.ShapeDtypeStruct((T, N), out_dtype),
        compiler_params=_params(("parallel", "parallel")),
        name="peer_query",
    )(a, w)


def peer_route(pq, keys, *, topk, tm):
    T = pq.shape[0]
    nheads, _, nkeys, kd = keys.shape
    kern = functools.partial(_route_kernel, nkeys=nkeys, topk=topk)
    ncand = sum(topk // (a + 1) for a in range(topk))
    ncand = -(-ncand // SUBLANES) * SUBLANES
    return pl.pallas_call(
        kern,
        grid=(T // tm,),
        in_specs=[
            pl.BlockSpec((tm, nheads * 2 * kd), lambda i: (i, 0)),
            pl.BlockSpec((nheads, 2, nkeys, kd), lambda i: (0, 0, 0, 0)),
        ],
        out_specs=[pl.BlockSpec((nheads * topk, tm), lambda i: (0, i)),
                   pl.BlockSpec((nheads * topk, tm), lambda i: (0, i))],
        out_shape=[jax.ShapeDtypeStruct((nheads * topk, T), I32),
                   jax.ShapeDtypeStruct((nheads * topk, T), F32)],
        scratch_shapes=[pltpu.VMEM((topk, 2 * tm), F32), pltpu.VMEM((topk, 2 * tm), I32),
                        pltpu.VMEM((ncand, tm), I32)],
        compiler_params=_params(("parallel",)),
        name="peer_route",
    )(pq, keys)


def _gelu(s):
    return 0.5 * s * (1.0 + lax.erf(s * (2.0 ** -0.5)))


def _expert_kernel(idx_cur_ref, idx_nxt_ref, tab_ref, x_ref, xn_ref, gate_ref, o_ref,
                   gbuf_a, gbuf_b, act_ref, sem_a, sem_b, *, nsel, tb, n_first):
    i = pl.program_id(0)
    nstep = pl.num_programs(0)
    rows = tb * nsel
    ngrp = nsel // SUBLANES
    nchunk = gbuf_a.shape[0]
    hi_mask = jnp.int32(-65536)
    gates = gate_ref[...]
    glane = lax.broadcasted_iota(I32, gates.shape, 1)
    tok0 = (i * 2 * tb) % gates.shape[1]

    def issue(idx_ref, idx_base, t, dst, sems, k_lo, k_hi):
        tok_idx = idx_ref.at[pl.ds(idx_base + t * nsel, nsel)]
        for kk in range(k_lo, k_hi):
            pltpu.make_async_copy(tab_ref.at[tok_idx[kk]], dst.at[:, t * nsel + kk, :],
                                  sems.at[t]).start(priority=kk % 2)

    def wait_token(buf, other, sems, t):
        rows_t = pl.ds(t * nsel, nsel)
        pltpu.make_async_copy(other.at[:, rows_t, :], buf.at[:, rows_t, :], sems.at[t]).wait()

    @pl.when(i == 0)
    def _():
        def body(t, carry):
            issue(idx_cur_ref, 0, t, gbuf_a, sem_a, 0, nsel)
            return carry
        lax.fori_loop(0, tb, body, 0)

    def phase(src, src_sems, dst, dst_sems, idx_ref, idx_base, tok_off):
        def score(t, acts):
            wait_token(src, dst, src_sems, t)
            issue(idx_ref, idx_base, t, dst, dst_sems, 0, n_first)
            xt = xn_ref[tok_off + t]
            xb = [jnp.broadcast_to(xt[c:c + 1, :], (SUBLANES, LANES)) for c in range(nchunk)]
            cols = []
            for gidx in range(ngrp):
                r0 = pl.multiple_of(t * nsel + gidx * SUBLANES, SUBLANES)
                acc = [jnp.zeros((SUBLANES, LANES), F32) for _ in range(2)]
                for c in range(nchunk):
                    uf = lax.bitcast_convert_type(src[c, pl.ds(r0, SUBLANES), :] << 16, F32)
                    acc[c % 2] = acc[c % 2] + uf * xb[c]
                cols.append(jnp.sum(acc[0] + acc[1], axis=1, keepdims=True))
            s = jnp.concatenate(cols, axis=0)
            return jnp.where(glane == tok0 + tok_off + t, s, acts)

        scores = lax.fori_loop(0, tb, score, jnp.zeros(gates.shape, F32))
        act_ref[...] = _gelu(scores) * gates

        def mix(t, carry):
            issue(idx_ref, idx_base, t, dst, dst_sems, n_first, nsel)
            a = jnp.sum(jnp.where(glane == tok0 + tok_off + t, act_ref[...], 0.0), axis=1, keepdims=True)
            accs = [jnp.zeros((SUBLANES, LANES), F32) for _ in range(nchunk)]
            for gidx in range(ngrp):
                r0 = pl.multiple_of(t * nsel + gidx * SUBLANES, SUBLANES)
                a_g = jnp.broadcast_to(a[gidx * SUBLANES:(gidx + 1) * SUBLANES, :], (SUBLANES, LANES))
                for c in range(nchunk):
                    vf = lax.bitcast_convert_type(src[c, pl.ds(r0, SUBLANES), :] & hi_mask, F32)
                    accs[c] = accs[c] + vf * a_g
            out = jnp.concatenate([jnp.sum(acc, axis=0, keepdims=True) for acc in accs], axis=0)
            o_ref[tok_off + t] = x_ref[tok_off + t] + out
            return carry

        lax.fori_loop(0, tb, mix, 0)

    phase(gbuf_a, sem_a, gbuf_b, sem_b, idx_cur_ref, rows, 0)
    phase(gbuf_b, sem_b, gbuf_a, sem_a, idx_nxt_ref, 0, tb)

    @pl.when(i == nstep - 1)
    def _():
        def body(t, carry):
            wait_token(gbuf_a, gbuf_b, sem_a, t)
            return carry
        lax.fori_loop(0, tb, body, 0)


def peer_experts(idx_flat, tab, x1_3, xn2_3, gate_t, *, nsel, tb):
    T, C, _ = x1_3.shape
    nstep = T // (2 * tb)
    rows = tb * nsel
    gblk = min(T, LANES)
    assert gblk % (2 * tb) == 0
    kern = functools.partial(_expert_kernel, nsel=nsel, tb=tb, n_first=(5 * nsel) // 8)
    return pl.pallas_call(
        kern,
        grid=(nstep,),
        in_specs=[
            pl.BlockSpec((2 * rows,), lambda i: (i,), memory_space=pltpu.SMEM),
            pl.BlockSpec((2 * rows,), lambda i: (jnp.minimum(i + 1, nstep - 1),), memory_space=pltpu.SMEM),
            pl.BlockSpec(memory_space=pl.ANY),
            pl.BlockSpec((2 * tb, C, LANES), lambda i: (i, 0, 0)),
            pl.BlockSpec((2 * tb, C, LANES), lambda i: (i, 0, 0)),
            pl.BlockSpec((nsel, gblk), lambda i: (0, (i * 2 * tb) // gblk)),
        ],
        out_specs=pl.BlockSpec((2 * tb, C, LANES), lambda i: (i, 0, 0)),
        out_shape=jax.ShapeDtypeStruct((T, C, LANES), F32),
        scratch_shapes=[pltpu.VMEM((C, rows, LANES), I32), pltpu.VMEM((C, rows, LANES), I32),
                        pltpu.VMEM((nsel, gblk), F32),
                        pltpu.SemaphoreType.DMA((tb,)), pltpu.SemaphoreType.DMA((tb,))],
        compiler_params=_params(("arbitrary",)),
        name="peer_experts",
    )(idx_flat, idx_flat, tab, x1_3, xn2_3, gate_t)


def _pack_kernel(u_ref, v_ref, o_ref):
    bu = lax.bitcast_convert_type(u_ref[...].astype(BF16).astype(F32), I32)
    bv = lax.bitcast_convert_type(v_ref[...].astype(BF16).astype(F32), I32)
    o_ref[...] = bv | lax.shift_right_logical(bu, 16)


def pack_expert_table(peer_u, peer_v, *, te=512):
    E, D = peer_u.shape
    return pl.pallas_call(
        _pack_kernel,
        grid=(E // te,),
        in_specs=[pl.BlockSpec((te, D), lambda i: (i, 0)), pl.BlockSpec((te, D), lambda i: (i, 0))],
        out_specs=pl.BlockSpec((te, D), lambda i: (i, 0)),
        out_shape=jax.ShapeDtypeStruct((E, D), I32),
        compiler_params=_params(("parallel",)),
        name="pack_experts",
    )(peer_u, peer_v)


def kernel(x, norm1_g, w_in, mlstm_gate_b, mlstm_norm_g, w_m_proj, attn_q_norm_g, attn_k_norm_g,
           attn_sink, rel_bias, w_a_proj, w_out, norm2_g, peer_wq, peer_keys, peer_u, peer_v):
    B, S, D = x.shape
    T = B * S
    mw = w_m_proj.shape[0]
    dh_m = mw // M_HEADS
    aw = w_a_proj.shape[0]
    kvw = A_KV_HEADS * A_DH
    ngate = 4 * M_HEADS

    splits = (mw, mw, mw, mw, ngate, aw, kvw, kvw, D, D)
    offs = [0]
    for sz in splits:
        offs.append(offs[-1] + sz)
    w_bf = w_in.astype(BF16)
    seg = lambda n: w_bf[:, offs[n]:offs[n + 1]]
    w_rest = jnp.concatenate([w_bf[:, :offs[4]], seg(8), seg(9), seg(5), seg(6), seg(7)], axis=1)
    wgt = w_in[:, offs[4]:offs[5]].T

    x2 = x.reshape(T, D)
    q_col = 4 * mw + 2 * D
    proj, gates_t = in_proj(x2, norm1_g.reshape(1, D), w_rest, wgt,
                            attn_q_norm_g.reshape(1, A_DH).astype(F32), attn_k_norm_g.reshape(1, A_DH).astype(F32),
                            tm=1024, tn=1024, q_col=q_col, k_col=q_col + aw, v_col=q_col + aw + kvw, dh=A_DH)

    gb = mlstm_gate_b.reshape(ngate, 1).astype(F32)
    hs = mlstm(proj, gates_t, gb, B=B, S=S, nh=M_HEADS, dh=dh_m, L=512)

    a_col0 = q_col // A_DH
    ha = attention(proj, attn_sink.astype(F32), attn_bias_table(rel_bias, WINDOW, WINDOW),
                   B=B, S=S, q_col=a_col0, k_col=a_col0 + A_HEADS, v_col=a_col0 + A_HEADS + A_KV_HEADS,
                   nkv=A_KV_HEADS, group=A_GROUP, dh=A_DH, blk=WINDOW)

    x1, xn2, xn2b = merge(hs, proj, ha, x2, mlstm_norm_g.reshape(1, mw), norm2_g.reshape(1, D),
                    w_m_proj.astype(BF16), w_a_proj.astype(BF16), w_out.astype(BF16),
                    nh=M_HEADS, mo_col=3, gm_col=(4 * mw) // D, ga_col=(4 * mw) // D + 1, tm=128)

    pq = matmul(xn2b, peer_wq.astype(BF16), out_dtype=BF16, tm=512, tn=1024)
    e_t, g_t = peer_route(pq, peer_keys.astype(BF16), topk=PEER_TOPK, tm=256)
    nsel = PEER_HEADS * PEER_TOPK
    idx_flat = e_t.T.reshape(T * nsel)
    C = D // LANES
    table = pack_expert_table(peer_u, peer_v).reshape(peer_u.shape[0], C, LANES)
    y = peer_experts(idx_flat, table, x1.reshape(T, C, LANES),
                     xn2.reshape(T, C, LANES), g_t, nsel=nsel, tb=16)
    return y.reshape(B, S, D)
```

```python
import functools
import math

import jax
import jax.numpy as jnp
from jax import lax
from jax.experimental import pallas as pl
from jax.experimental.pallas import tpu as pltpu

F32 = jnp.float32
BF16 = jnp.bfloat16
I32 = jnp.int32
HIGHEST = lax.Precision.HIGHEST
EPS = 1e-6
LANES = 128
SUBLANES = 8
VMEM_LIMIT = 56 * 1024 * 1024

M_HEADS = 4
A_HEADS = 16
A_KV_HEADS = 4
A_GROUP = A_HEADS // A_KV_HEADS
A_DH = 128
WINDOW = 128
REL_BUCKETS = 32
REL_MAX_DIST = 128
PEER_HEADS = 8
PEER_NKEYS = 128
PEER_TOPK = 16


def _params(sem, vmem=VMEM_LIMIT):
    return pltpu.CompilerParams(dimension_semantics=sem, vmem_limit_bytes=vmem)


def _log_sigmoid(x):
    return jnp.minimum(x, 0.0) - jnp.log1p(jnp.exp(-jnp.abs(x)))


def _head_rms(t, gain, dh):
    parts = []
    for h in range(t.shape[1] // dh):
        seg = t[:, h * dh:(h + 1) * dh]
        parts.append(seg * lax.rsqrt(jnp.mean(seg * seg, axis=-1, keepdims=True) + EPS) * gain)
    return jnp.concatenate(parts, axis=1)


def _in_proj_kernel(x_ref, g_ref, w_ref, wgt_ref, qg_ref, kg_ref, o_ref, ogt_ref, xn_ref,
                    *, q_blk, k_blk, k_cols, dh, norm_rows):
    j = pl.program_id(1)

    @pl.when(j == 0)
    def _():
        for r0 in range(0, x_ref.shape[0], norm_rows):
            rows = slice(r0, r0 + norm_rows)
            x = x_ref[rows, :]
            y = x * lax.rsqrt(jnp.mean(x * x, axis=-1, keepdims=True) + EPS) * g_ref[...]
            yb = y.astype(BF16)
            xn_ref[rows, :] = yb
            ogt_ref[:, rows] = lax.dot_general(wgt_ref[...].astype(BF16), yb, (((1,), (1,)), ((), ())),
                                               preferred_element_type=F32)

    acc = jnp.dot(xn_ref[...], w_ref[...], preferred_element_type=F32)

    @pl.when(j < q_blk)
    def _():
        o_ref[...] = acc.astype(o_ref.dtype)

    @pl.when((j >= q_blk) & (j < k_blk))
    def _():
        o_ref[...] = _head_rms(acc, qg_ref[...] * (dh ** -0.5), dh).astype(o_ref.dtype)

    @pl.when(j == k_blk)
    def _():
        o_ref[...] = jnp.concatenate([_head_rms(acc[:, :k_cols], kg_ref[...], dh), acc[:, k_cols:]],
                                     axis=1).astype(o_ref.dtype)


def in_proj(x2, g, w, wgt, qg, kg, *, tm, tn, q_col, k_col, v_col, dh):
    T, D = x2.shape
    N = w.shape[1]
    ng = wgt.shape[0]
    assert q_col % tn == 0 and k_col % tn == 0 and N - k_col == tn and (v_col - k_col) % dh == 0
    kern = functools.partial(_in_proj_kernel, q_blk=q_col // tn, k_blk=k_col // tn,
                             k_cols=v_col - k_col, dh=dh, norm_rows=min(tm, 256))
    return pl.pallas_call(
        kern,
        grid=(T // tm, N // tn),
        in_specs=[
            pl.BlockSpec((tm, D), lambda i, j: (i, 0)),
            pl.BlockSpec((1, D), lambda i, j: (0, 0)),
            pl.BlockSpec((D, tn), lambda i, j: (0, j)),
            pl.BlockSpec((ng, D), lambda i, j: (0, 0)),
            pl.BlockSpec((1, dh), lambda i, j: (0, 0)),
            pl.BlockSpec((1, dh), lambda i, j: (0, 0)),
        ],
        out_specs=[
            pl.BlockSpec((tm, tn), lambda i, j: (i, j)),
            pl.BlockSpec((ng, tm), lambda i, j: (0, i)),
        ],
        out_shape=[
            jax.ShapeDtypeStruct((T, N), BF16),
            jax.ShapeDtypeStruct((ng, T), F32),
        ],
        scratch_shapes=[pltpu.VMEM((tm, D), BF16)],
        compiler_params=_params(("parallel", "arbitrary")),
        name="in_proj",
    )(x2, g, w, wgt, qg, kg)


def _mlstm_kernel(q_ref, k_ref, v_ref, gt_ref, bcol_ref, h_ref,
                  c_ref, n_ref, m_ref, *, nh, scale):
    d = pl.program_id(0)
    hh = pl.program_id(2)
    c = pl.program_id(3)
    L = q_ref.shape[0]

    @pl.when(c == 0)
    def _():
        c_ref[...] = jnp.zeros_like(c_ref)
        n_ref[...] = jnp.zeros_like(n_ref)
        m_ref[...] = jnp.zeros_like(m_ref)

    col_i = d * 2 * nh + hh
    col_f = col_i + nh
    fwd = d == 0

    r_idx = lax.broadcasted_iota(I32, (L, L), 0)
    s_idx = lax.broadcasted_iota(I32, (L, L), 1)
    seen = (r_idx - s_idx) * (1 - 2 * d) >= 0
    tri = seen.astype(F32)

    grow = gt_ref[...] + bcol_ref[...]
    lf_rows = _log_sigmoid(grow)
    b_rows = lax.dot_general(lf_rows, tri, (((1,), (1,)), ((), ())),
                             precision=HIGHEST, preferred_element_type=F32)
    sub = lax.broadcasted_iota(I32, grow.shape, 0)
    pick_row = lambda a, ridx: jnp.sum(jnp.where(sub == ridx, a, 0.0), axis=0, keepdims=True)
    li_row = pick_row(grow, col_i)
    b_row = pick_row(b_rows, col_f)
    cols = jnp.concatenate([li_row, b_row, jnp.zeros((LANES - 2, L), F32)], axis=0).T
    li_col = cols[:, 0:1]
    b_col = cols[:, 1:2]

    m_prev = m_ref[...]
    dmat = jnp.where(seen, b_col - b_row + li_row, -jnp.inf)
    inter = b_col + m_prev
    m_row = jnp.maximum(inter, jnp.max(dmat, axis=1, keepdims=True))
    w_inter = jnp.exp(inter - m_row)

    q = q_ref[...]
    k = k_ref[...]
    v = v_ref[...]
    qk = lax.dot_general(q, k, (((1,), (1,)), ((), ())), preferred_element_type=F32)
    s = qk * (jnp.exp(dmat - m_row) * scale)
    qf = q.astype(F32)
    num = (w_inter * jnp.dot(q, c_ref[...].astype(BF16), preferred_element_type=F32)
           + jnp.dot(s.astype(BF16), v, preferred_element_type=F32))
    den = (w_inter * jnp.sum(qf * n_ref[...], axis=1, keepdims=True)
           + jnp.sum(s, axis=1, keepdims=True))
    h_ref[...] = num / jnp.maximum(jnp.abs(den), jnp.exp(-m_row))

    b_last = jnp.max(jnp.where(fwd, b_col[L - 1:L, :], b_col[0:1, :]), axis=0, keepdims=True)
    g_col = b_last - b_col + li_col
    m_new = jnp.maximum(b_last + m_prev, jnp.max(g_col, axis=0, keepdims=True))
    decay = jnp.exp(b_last + m_prev - m_new)
    wk = jnp.exp(g_col - m_new) * scale
    kw = k.astype(F32) * wk
    c_ref[...] = decay * c_ref[...] + lax.dot_general(
        kw.astype(BF16), v, (((0,), (0,)), ((), ())), preferred_element_type=F32)
    n_ref[...] = decay * n_ref[...] + jnp.sum(kw, axis=0, keepdims=True)
    m_ref[...] = m_new


def mlstm(proj, gates_t, bias_col, *, B, S, nh, dh, L):
    nc = S // L
    T = B * S
    ng = gates_t.shape[0]

    def row(d, b, c):
        return b * nc + c + d * (nc - 1 - 2 * c)

    kern = functools.partial(_mlstm_kernel, nh=nh, scale=dh ** -0.5)
    return pl.pallas_call(
        kern,
        grid=(2, B, nh, nc),
        in_specs=[
            pl.BlockSpec((L, dh), lambda d, b, h, c: (row(d, b, c), h)),
            pl.BlockSpec((L, dh), lambda d, b, h, c: (row(d, b, c), nh + h)),
            pl.BlockSpec((L, dh), lambda d, b, h, c: (row(d, b, c), 2 * nh + h)),
            pl.BlockSpec((ng, L), lambda d, b, h, c: (0, row(d, b, c))),
            pl.BlockSpec((ng, 1), lambda d, b, h, c: (0, 0)),
        ],
        out_specs=pl.BlockSpec((None, L, dh), lambda d, b, h, c: (d, row(d, b, c), h)),
        out_shape=jax.ShapeDtypeStruct((2, T, nh * dh), F32),
        scratch_shapes=[pltpu.VMEM((dh, dh), F32), pltpu.VMEM((1, dh), F32), pltpu.VMEM((1, 1), F32)],
        compiler_params=_params(("parallel", "parallel", "parallel", "arbitrary")),
        name="mlstm",
    )(proj, proj, proj, gates_t, bias_col)


def _attn_kernel(sink_ref, q_ref, k0_ref, k1_ref, k2_ref, v0_ref, v1_ref, v2_ref,
                 bias_ref, o_ref, *, group, dh):
    i = pl.program_id(1)
    nb = pl.num_programs(1)
    blk = q_ref.shape[0]
    nkv = k0_ref.shape[1] // dh

    lane = lax.broadcasted_iota(I32, (1, 3 * blk), 1)
    ok = ((lane >= blk) | (i > 0)) & ((lane < 2 * blk) | (i < nb - 1))
    for g in range(nkv):
        cols = slice(g * dh, (g + 1) * dh)
        kcat = jnp.concatenate([k0_ref[:, cols], k1_ref[:, cols], k2_ref[:, cols]], axis=0)
        vcat = jnp.concatenate([v0_ref[:, cols], v1_ref[:, cols], v2_ref[:, cols]], axis=0)
        for hq in range(g * group, (g + 1) * group):
            s = lax.dot_general(q_ref[:, hq * dh:(hq + 1) * dh], kcat, (((1,), (1,)), ((), ())),
                                preferred_element_type=F32)
            s = jnp.where(ok, s + bias_ref[hq], -jnp.inf)
            sink = sink_ref[hq]
            m = jnp.maximum(jnp.max(s, axis=1, keepdims=True), sink)
            p = jnp.exp(s - m)
            den = jnp.sum(p, axis=1, keepdims=True) + jnp.exp(sink - m)
            o = jnp.dot(p.astype(BF16), vcat, preferred_element_type=F32) / den
            o_ref[:, hq * dh:(hq + 1) * dh] = o.astype(o_ref.dtype)


def attention(proj, sink, bias_tab, *, B, S, q_col, k_col, v_col, nkv, group, dh, blk):
    nb = S // blk
    T = B * S
    qw = nkv * group * dh
    kvw = nkv * dh
    assert (q_col * dh) % qw == 0 and (k_col * dh) % kvw == 0 and (v_col * dh) % kvw == 0

    def kv_spec(col0, off):
        def imap(b, i, sink_ref):
            j = jnp.clip(i + off, 0, nb - 1)
            return (b * nb + j, (col0 * dh) // kvw)
        return pl.BlockSpec((blk, kvw), imap)

    kern = functools.partial(_attn_kernel, group=group, dh=dh)
    grid_spec = pltpu.PrefetchScalarGridSpec(
        num_scalar_prefetch=1,
        grid=(B, nb),
        in_specs=[
            pl.BlockSpec((blk, qw), lambda b, i, s: (b * nb + i, (q_col * dh) // qw)),
            kv_spec(k_col, -1), kv_spec(k_col, 0), kv_spec(k_col, 1),
            kv_spec(v_col, -1), kv_spec(v_col, 0), kv_spec(v_col, 1),
            pl.BlockSpec((nkv * group, blk, 3 * blk), lambda b, i, s: (0, 0, 0)),
        ],
        out_specs=pl.BlockSpec((blk, qw), lambda b, i, s: (b * nb + i, 0)),
    )
    return pl.pallas_call(
        kern,
        grid_spec=grid_spec,
        out_shape=jax.ShapeDtypeStruct((T, qw), BF16),
        compiler_params=_params(("parallel", "arbitrary")),
        name="window_attn",
    )(sink, proj, proj, proj, proj, proj, proj, proj, bias_tab)


def _t5_bucket(rel):
    half = REL_BUCKETS // 2
    max_exact = half // 2
    ret = jnp.where(rel > 0, half, 0)
    n = jnp.abs(rel)
    nf = jnp.maximum(n, 1).astype(F32)
    large = max_exact + (jnp.log(nf / max_exact) / math.log(REL_MAX_DIST / max_exact)
                         * (half - max_exact)).astype(I32)
    large = jnp.minimum(large, half - 1)
    return ret + jnp.where(n < max_exact, n, large)


def attn_bias_table(rel_bias, blk, window):
    q_local = jnp.arange(blk)
    k_local = jnp.arange(3 * blk) - blk
    rel = k_local[None, :] - q_local[:, None]
    offs = jnp.arange(-(blk - 1), 3 * blk) - blk
    hit = _t5_bucket(offs)[:, None] == jnp.arange(REL_BUCKETS)
    by_off = jnp.sum(jnp.where(hit[None], rel_bias.astype(F32).T[:, None, :], 0.0), axis=-1)
    bias = jnp.stack([by_off[:, blk - 1 - q: blk - 1 - q + 3 * blk] for q in range(blk)], axis=1)
    return jnp.where((jnp.abs(rel) <= window)[None], bias, -jnp.inf)


def _merge_kernel(hs_ref, mo_ref, gm_ref, ga_ref, ha_ref, x_ref, ng_ref, n2_ref,
                  wm_ref, wa_ref, wo_ref, x1_ref, xn2_ref, xn2b_ref, *, nh):
    hsum = hs_ref[0] + hs_ref[1]
    dh = hsum.shape[1] // nh
    segs = []
    for hh in range(nh):
        seg = hsum[:, hh * dh:(hh + 1) * dh]
        segs.append(seg * lax.rsqrt(jnp.mean(seg * seg, axis=-1, keepdims=True) + EPS))
    hm = jnp.concatenate(segs, axis=1) * ng_ref[...] * jax.nn.sigmoid(mo_ref[...].astype(F32))
    br_m = jnp.dot(hm.astype(BF16), wm_ref[...], preferred_element_type=F32)
    br_a = jnp.dot(ha_ref[...], wa_ref[...], preferred_element_type=F32)
    merged = (jax.nn.sigmoid(gm_ref[...].astype(F32)) * br_m
              + jax.nn.sigmoid(ga_ref[...].astype(F32)) * br_a)
    x1 = x_ref[...] + jnp.dot(merged.astype(BF16), wo_ref[...], preferred_element_type=F32)
    x1_ref[...] = x1
    xn2 = x1 * lax.rsqrt(jnp.mean(x1 * x1, axis=-1, keepdims=True) + EPS) * n2_ref[...]
    xn2_ref[...] = xn2
    xn2b_ref[...] = xn2.astype(BF16)


def merge(hs, proj, ha, x2, ng, n2, wm, wa, wo, *, nh, mo_col, gm_col, ga_col, tm):
    T, D = x2.shape
    W = hs.shape[2]
    const = lambda shape: pl.BlockSpec(shape, lambda i: (0,) * len(shape), pipeline_mode=pl.Buffered(1))
    kern = functools.partial(_merge_kernel, nh=nh)
    return pl.pallas_call(
        kern,
        grid=(T // tm,),
        in_specs=[
            pl.BlockSpec((2, tm, W), lambda i: (0, i, 0)),
            pl.BlockSpec((tm, W), lambda i: (i, mo_col)),
            pl.BlockSpec((tm, D), lambda i: (i, gm_col)),
            pl.BlockSpec((tm, D), lambda i: (i, ga_col)),
            pl.BlockSpec((tm, ha.shape[1]), lambda i: (i, 0)),
            pl.BlockSpec((tm, D), lambda i: (i, 0)),
            const((1, W)), const((1, D)),
            const(wm.shape), const(wa.shape), const(wo.shape),
        ],
        out_specs=[pl.BlockSpec((tm, D), lambda i: (i, 0))] * 3,
        out_shape=[jax.ShapeDtypeStruct((T, D), F32), jax.ShapeDtypeStruct((T, D), F32),
                   jax.ShapeDtypeStruct((T, D), BF16)],
        compiler_params=_params(("parallel",)),
        name="merge",
    )(hs, proj, proj, proj, ha, x2, ng, n2, wm, wa, wo)


def _extract_top(vals, n_take, emit):
    rows = vals.shape[0]
    ridx = lax.broadcasted_iota(I32, vals.shape, 0)

    def body(kk, cur):
        mx = jnp.max(cur, axis=0, keepdims=True)
        idx = jnp.min(jnp.where(cur == mx, ridx, rows), axis=0, keepdims=True)
        hit = ridx == idx
        emit(kk, mx, idx, hit)
        return jnp.where(hit, -jnp.inf, cur)

    lax.fori_loop(0, n_take, body, vals)


def _route_kernel(pq_ref, keys_ref, e_ref, gate_ref, sv_ref, si_ref, pe_ref, *, nkeys, topk):
    tm = pq_ref.shape[0]
    nheads = keys_ref.shape[0]
    qh = pq_ref.shape[1] // (2 * nheads)
    width = [topk // (a + 1) for a in range(topk)]
    pad = pe_ref.shape[0] - sum(width)
    for h in range(nheads):
        for p in range(2):
            c0 = (2 * h + p) * qh
            sub = lax.dot_general(keys_ref[h, p], pq_ref[:, c0:c0 + qh],
                                  (((1,), (1,)), ((), ())), preferred_element_type=F32)

            def emit(kk, mx, idx, hit, p=p):
                sv_ref[pl.ds(kk, 1), p * tm:(p + 1) * tm] = mx
                si_ref[pl.ds(kk, 1), p * tm:(p + 1) * tm] = idx

            _extract_top(sub, topk, emit)

        sv0, sv1 = sv_ref[:, :tm], sv_ref[:, tm:]
        si0, si1 = si_ref[:, :tm], si_ref[:, tm:]
        cand = jnp.concatenate([sv0[a:a + 1, :] + sv1[:width[a], :] for a in range(topk)]
                               + [jnp.full((pad, tm), -jnp.inf, F32)], axis=0)
        pe_ref[...] = jnp.concatenate([si0[a:a + 1, :] * nkeys + si1[:width[a], :] for a in range(topk)]
                                      + [jnp.full((pad, tm), -1, I32)], axis=0)
        r0 = h * topk

        def emit2(kk, mx, idx, hit, r0=r0):
            gate_ref[pl.ds(r0 + kk, 1), :] = mx
            e_ref[pl.ds(r0 + kk, 1), :] = jnp.max(jnp.where(hit, pe_ref[...], -1), axis=0, keepdims=True)

        _extract_top(cand, topk, emit2)
        cv = gate_ref[r0:r0 + topk, :]
        ex = jnp.exp(cv - cv[0:1, :])
        gate_ref[r0:r0 + topk, :] = ex / jnp.sum(ex, axis=0, keepdims=True)


def _matmul_kernel(a_ref, w_ref, o_ref):
    o_ref[...] = jnp.dot(a_ref[...], w_ref[...], preferred_element_type=F32).astype(o_ref.dtype)


def matmul(a, w, *, out_dtype, tm, tn):
    T, K = a.shape
    N = w.shape[1]
    return pl.pallas_call(
        _matmul_kernel,
        grid=(T // tm, N // tn),
        in_specs=[pl.BlockSpec((tm, K), lambda i, j: (i, 0)), pl.BlockSpec((K, tn), lambda i, j: (0, j))],
        out_specs=pl.BlockSpec((tm, tn), lambda i, j: (i, j)),
        out_shape=jax.ShapeDtypeStruct((T, N), out_dtype),
        compiler_params=_params(("parallel", "parallel")),
        name="peer_query",
    )(a, w)


def peer_route(pq, keys, *, topk, tm):
    T = pq.shape[0]
    nheads, _, nkeys, kd = keys.shape
    kern = functools.partial(_route_kernel, nkeys=nkeys, topk=topk)
    ncand = sum(topk // (a + 1) for a in range(topk))
    ncand = -(-ncand // SUBLANES) * SUBLANES
    return pl.pallas_call(
        kern,
        grid=(T // tm,),
        in_specs=[
            pl.BlockSpec((tm, nheads * 2 * kd), lambda i: (i, 0)),
            pl.BlockSpec((nheads, 2, nkeys, kd), lambda i: (0, 0, 0, 0)),
        ],
        out_specs=[pl.BlockSpec((nheads * topk, tm), lambda i: (0, i)),
                   pl.BlockSpec((nheads * topk, tm), lambda i: (0, i))],
        out_shape=[jax.ShapeDtypeStruct((nheads * topk, T), I32),
                   jax.ShapeDtypeStruct((nheads * topk, T), F32)],
        scratch_shapes=[pltpu.VMEM((topk, 2 * tm), F32), pltpu.VMEM((topk, 2 * tm), I32),
                        pltpu.VMEM((ncand, tm), I32)],
        compiler_params=_params(("parallel",)),
        name="peer_route",
    )(pq, keys)


def _gelu(s):
    return 0.5 * s * (1.0 + lax.erf(s * (2.0 ** -0.5)))


def _expert_kernel(idx_cur_ref, idx_nxt_ref, tab_ref, x_ref, xn_ref, gate_ref, o_ref,
                   gbuf_a, gbuf_b, act_ref, sem_a, sem_b, *, nsel, tb, n_first):
    i = pl.program_id(0)
    nstep = pl.num_programs(0)
    rows = tb * nsel
    ngrp = nsel // SUBLANES
    nchunk = gbuf_a.shape[0]
    hi_mask = jnp.int32(-65536)
    gates = gate_ref[...]
    glane = lax.broadcasted_iota(I32, gates.shape, 1)
    tok0 = (i * 2 * tb) % gates.shape[1]

    def issue(idx_ref, idx_base, t, dst, sems, k_lo, k_hi):
        tok_idx = idx_ref.at[pl.ds(idx_base + t * nsel, nsel)]
        for kk in range(k_lo, k_hi):
            pltpu.make_async_copy(tab_ref.at[tok_idx[kk]], dst.at[:, t * nsel + kk, :],
                                  sems.at[t]).start(priority=kk % 2)

    def wait_token(buf, other, sems, t):
        rows_t = pl.ds(t * nsel, nsel)
        pltpu.make_async_copy(other.at[:, rows_t, :], buf.at[:, rows_t, :], sems.at[t]).wait()

    @pl.when(i == 0)
    def _():
        def body(t, carry):
            issue(idx_cur_ref, 0, t, gbuf_a, sem_a, 0, nsel)
            return carry
        lax.fori_loop(0, tb, body, 0)

    def phase(src, src_sems, dst, dst_sems, idx_ref, idx_base, tok_off):
        def score(t, acts):
            wait_token(src, dst, src_sems, t)
            issue(idx_ref, idx_base, t, dst, dst_sems, 0, n_first)
            xt = xn_ref[tok_off + t]
            xb = [jnp.broadcast_to(xt[c:c + 1, :], (SUBLANES, LANES)) for c in range(nchunk)]
            cols = []
            for gidx in range(ngrp):
                r0 = pl.multiple_of(t * nsel + gidx * SUBLANES, SUBLANES)
                acc = [jnp.zeros((SUBLANES, LANES), F32) for _ in range(2)]
                for c in range(nchunk):
                    uf = lax.bitcast_convert_type(src[c, pl.ds(r0, SUBLANES), :] << 16, F32)
                    acc[c % 2] = acc[c % 2] + uf * xb[c]
                cols.append(jnp.sum(acc[0] + acc[1], axis=1, keepdims=True))
            s = jnp.concatenate(cols, axis=0)
            return jnp.where(glane == tok0 + tok_off + t, s, acts)

        scores = lax.fori_loop(0, tb, score, jnp.zeros(gates.shape, F32))
        act_ref[...] = _gelu(scores) * gates

        def mix(t, carry):
            issue(idx_ref, idx_base, t, dst, dst_sems, n_first, nsel)
            a = jnp.sum(jnp.where(glane == tok0 + tok_off + t, act_ref[...], 0.0), axis=1, keepdims=True)
            accs = [jnp.zeros((SUBLANES, LANES), F32) for _ in range(nchunk)]
            for gidx in range(ngrp):
                r0 = pl.multiple_of(t * nsel + gidx * SUBLANES, SUBLANES)
                a_g = jnp.broadcast_to(a[gidx * SUBLANES:(gidx + 1) * SUBLANES, :], (SUBLANES, LANES))
                for c in range(nchunk):
                    vf = lax.bitcast_convert_type(src[c, pl.ds(r0, SUBLANES), :] & hi_mask, F32)
                    accs[c] = accs[c] + vf * a_g
            out = jnp.concatenate([jnp.sum(acc, axis=0, keepdims=True) for acc in accs], axis=0)
            o_ref[tok_off + t] = x_ref[tok_off + t] + out
            return carry

        lax.fori_loop(0, tb, mix, 0)

    phase(gbuf_a, sem_a, gbuf_b, sem_b, idx_cur_ref, rows, 0)
    phase(gbuf_b, sem_b, gbuf_a, sem_a, idx_nxt_ref, 0, tb)

    @pl.when(i == nstep - 1)
    def _():
        def body(t, carry):
            wait_token(gbuf_a, gbuf_b, sem_a, t)
            return carry
        lax.fori_loop(0, tb, body, 0)


def peer_experts(idx_flat, tab, x1_3, xn2_3, gate_t, *, nsel, tb):
    T, C, _ = x1_3.shape
    nstep = T // (2 * tb)
    rows = tb * nsel
    gblk = min(T, LANES)
    assert gblk % (2 * tb) == 0
    kern = functools.partial(_expert_kernel, nsel=nsel, tb=tb, n_first=(5 * nsel) // 8)
    return pl.pallas_call(
        kern,
        grid=(nstep,),
        in_specs=[
            pl.BlockSpec((2 * rows,), lambda i: (i,), memory_space=pltpu.SMEM),
            pl.BlockSpec((2 * rows,), lambda i: (jnp.minimum(i + 1, nstep - 1),), memory_space=pltpu.SMEM),
            pl.BlockSpec(memory_space=pl.ANY),
            pl.BlockSpec((2 * tb, C, LANES), lambda i: (i, 0, 0)),
            pl.BlockSpec((2 * tb, C, LANES), lambda i: (i, 0, 0)),
            pl.BlockSpec((nsel, gblk), lambda i: (0, (i * 2 * tb) // gblk)),
        ],
        out_specs=pl.BlockSpec((2 * tb, C, LANES), lambda i: (i, 0, 0)),
        out_shape=jax.ShapeDtypeStruct((T, C, LANES), F32),
        scratch_shapes=[pltpu.VMEM((C, rows, LANES), I32), pltpu.VMEM((C, rows, LANES), I32),
                        pltpu.VMEM((nsel, gblk), F32),
                        pltpu.SemaphoreType.DMA((tb,)), pltpu.SemaphoreType.DMA((tb,))],
        compiler_params=_params(("arbitrary",)),
        name="peer_experts",
    )(idx_flat, idx_flat, tab, x1_3, xn2_3, gate_t)


def _pack_kernel(u_ref, v_ref, o_ref):
    bu = lax.bitcast_convert_type(u_ref[...].astype(BF16).astype(F32), I32)
    bv = lax.bitcast_convert_type(v_ref[...].astype(BF16).astype(F32), I32)
    o_ref[...] = bv | lax.shift_right_logical(bu, 16)


def pack_expert_table(peer_u, peer_v, *, te=512):
    E, D = peer_u.shape
    return pl.pallas_call(
        _pack_kernel,
        grid=(E // te,),
        in_specs=[pl.BlockSpec((te, D), lambda i: (i, 0)), pl.BlockSpec((te, D), lambda i: (i, 0))],
        out_specs=pl.BlockSpec((te, D), lambda i: (i, 0)),
        out_shape=jax.ShapeDtypeStruct((E, D), I32),
        compiler_params=_params(("parallel",)),
        name="pack_experts",
    )(peer_u, peer_v)


def kernel(x, norm1_g, w_in, mlstm_gate_b, mlstm_norm_g, w_m_proj, attn_q_norm_g, attn_k_norm_g,
           attn_sink, rel_bias, w_a_proj, w_out, norm2_g, peer_wq, peer_keys, peer_u, peer_v):
    B, S, D = x.shape
    T = B * S
    mw = w_m_proj.shape[0]
    dh_m = mw // M_HEADS
    aw = w_a_proj.shape[0]
    kvw = A_KV_HEADS * A_DH
    ngate = 4 * M_HEADS

    splits = (mw, mw, mw, mw, ngate, aw, kvw, kvw, D, D)
    offs = [0]
    for sz in splits:
        offs.append(offs[-1] + sz)
    w_bf = w_in.astype(BF16)
    seg = lambda n: w_bf[:, offs[n]:offs[n + 1]]
    w_rest = jnp.concatenate([w_bf[:, :offs[4]], seg(8), seg(9), seg(5), seg(6), seg(7)], axis=1)
    wgt = w_in[:, offs[4]:offs[5]].T

    x2 = x.reshape(T, D)
    q_col = 4 * mw + 2 * D
    proj, gates_t = in_proj(x2, norm1_g.reshape(1, D), w_rest, wgt,
                            attn_q_norm_g.reshape(1, A_DH).astype(F32), attn_k_norm_g.reshape(1, A_DH).astype(F32),
                            tm=1024, tn=1024, q_col=q_col, k_col=q_col + aw, v_col=q_col + aw + kvw, dh=A_DH)

    gb = mlstm_gate_b.reshape(ngate, 1).astype(F32)
    hs = mlstm(proj, gates_t, gb, B=B, S=S, nh=M_HEADS, dh=dh_m, L=512)

    a_col0 = q_col // A_DH
    ha = attention(proj, attn_sink.astype(F32), attn_bias_table(rel_bias, WINDOW, WINDOW),
                   B=B, S=S, q_col=a_col0, k_col=a_col0 + A_HEADS, v_col=a_col0 + A_HEADS + A_KV_HEADS,
                   nkv=A_KV_HEADS, group=A_GROUP, dh=A_DH, blk=WINDOW)

    x1, xn2, xn2b = merge(hs, proj, ha, x2, mlstm_norm_g.reshape(1, mw), norm2_g.reshape(1, D),
                    w_m_proj.astype(BF16), w_a_proj.astype(BF16), w_out.astype(BF16),
                    nh=M_HEADS, mo_col=3, gm_col=(4 * mw) // D, ga_col=(4 * mw) // D + 1, tm=128)

    pq = matmul(xn2b, peer_wq.astype(BF16), out_dtype=BF16, tm=512, tn=1024)
    e_t, g_t = peer_route(pq, peer_keys.astype(BF16), topk=PEER_TOPK, tm=256)
    nsel = PEER_HEADS * PEER_TOPK
    idx_flat = e_t.T.reshape(T * nsel)
    C = D // LANES
    table = pack_expert_table(peer_u, peer_v).reshape(peer_u.shape[0], C, LANES)
    y = peer_experts(idx_flat, table, x1.reshape(T, C, LANES),
                     xn2.reshape(T, C, LANES), g_t, nsel=nsel, tb=16)
    return y.reshape(B, S, D)
```

```python
import functools
import math

import jax
import jax.numpy as jnp
from jax import lax
from jax.experimental import pallas as pl
from jax.experimental.pallas import tpu as pltpu

F32 = jnp.float32
BF16 = jnp.bfloat16
I32 = jnp.int32
HIGHEST = lax.Precision.HIGHEST
EPS = 1e-6
LANES = 128
SUBLANES = 8
VMEM_LIMIT = 56 * 1024 * 1024

M_HEADS = 4
A_HEADS = 16
A_KV_HEADS = 4
A_GROUP = A_HEADS // A_KV_HEADS
A_DH = 128
WINDOW = 128
REL_BUCKETS = 32
REL_MAX_DIST = 128
PEER_HEADS = 8
PEER_NKEYS = 128
PEER_TOPK = 16


def _params(sem, vmem=VMEM_LIMIT):
    return pltpu.CompilerParams(dimension_semantics=sem, vmem_limit_bytes=vmem)


def _log_sigmoid(x):
    return jnp.minimum(x, 0.0) - jnp.log1p(jnp.exp(-jnp.abs(x)))


def _head_rms(t, gain, dh):
    parts = []
    for h in range(t.shape[1] // dh):
        seg = t[:, h * dh:(h + 1) * dh]
        parts.append(seg * lax.rsqrt(jnp.mean(seg * seg, axis=-1, keepdims=True) + EPS) * gain)
    return jnp.concatenate(parts, axis=1)


def _in_proj_kernel(x_ref, g_ref, w_ref, wgt_ref, qg_ref, kg_ref, o_ref, ogt_ref, xn_ref,
                    *, q_blk, k_blk, k_cols, dh, norm_rows):
    j = pl.program_id(1)

    @pl.when(j == 0)
    def _():
        for r0 in range(0, x_ref.shape[0], norm_rows):
            rows = slice(r0, r0 + norm_rows)
            x = x_ref[rows, :]
            y = x * lax.rsqrt(jnp.mean(x * x, axis=-1, keepdims=True) + EPS) * g_ref[...]
            yb = y.astype(BF16)
            xn_ref[rows, :] = yb
            ogt_ref[:, rows] = lax.dot_general(wgt_ref[...].astype(BF16), yb, (((1,), (1,)), ((), ())),
                                               preferred_element_type=F32)

    acc = jnp.dot(xn_ref[...], w_ref[...], preferred_element_type=F32)

    @pl.when(j < q_blk)
    def _():
        o_ref[...] = acc.astype(o_ref.dtype)

    @pl.when((j >= q_blk) & (j < k_blk))
    def _():
        o_ref[...] = _head_rms(acc, qg_ref[...] * (dh ** -0.5), dh).astype(o_ref.dtype)

    @pl.when(j == k_blk)
    def _():
        o_ref[...] = jnp.concatenate([_head_rms(acc[:, :k_cols], kg_ref[...], dh), acc[:, k_cols:]],
                                     axis=1).astype(o_ref.dtype)


def in_proj(x2, g, w, wgt, qg, kg, *, tm, tn, q_col, k_col, v_col, dh):
    T, D = x2.shape
    N = w.shape[1]
    ng = wgt.shape[0]
    assert q_col % tn == 0 and k_col % tn == 0 and N - k_col == tn and (v_col - k_col) % dh == 0
    kern = functools.partial(_in_proj_kernel, q_blk=q_col // tn, k_blk=k_col // tn,
                             k_cols=v_col - k_col, dh=dh, norm_rows=min(tm, 256))
    return pl.pallas_call(
        kern,
        grid=(T // tm, N // tn),
        in_specs=[
            pl.BlockSpec((tm, D), lambda i, j: (i, 0)),
            pl.BlockSpec((1, D), lambda i, j: (0, 0)),
            pl.BlockSpec((D, tn), lambda i, j: (0, j)),
            pl.BlockSpec((ng, D), lambda i, j: (0, 0)),
            pl.BlockSpec((1, dh), lambda i, j: (0, 0)),
            pl.BlockSpec((1, dh), lambda i, j: (0, 0)),
        ],
        out_specs=[
            pl.BlockSpec((tm, tn), lambda i, j: (i, j)),
            pl.BlockSpec((ng, tm), lambda i, j: (0, i)),
        ],
        out_shape=[
            jax.ShapeDtypeStruct((T, N), BF16),
            jax.ShapeDtypeStruct((ng, T), F32),
        ],
        scratch_shapes=[pltpu.VMEM((tm, D), BF16)],
        compiler_params=_params(("parallel", "arbitrary")),
        name="in_proj",
    )(x2, g, w, wgt, qg, kg)


def _mlstm_kernel(q_ref, k_ref, v_ref, gt_ref, bcol_ref, h_ref,
                  c_ref, n_ref, m_ref, *, nh, scale):
    d = pl.program_id(0)
    hh = pl.program_id(2)
    c = pl.program_id(3)
    L = q_ref.shape[0]

    @pl.when(c == 0)
    def _():
        c_ref[...] = jnp.zeros_like(c_ref)
        n_ref[...] = jnp.zeros_like(n_ref)
        m_ref[...] = jnp.zeros_like(m_ref)

    col_i = d * 2 * nh + hh
    col_f = col_i + nh
    fwd = d == 0

    r_idx = lax.broadcasted_iota(I32, (L, L), 0)
    s_idx = lax.broadcasted_iota(I32, (L, L), 1)
    seen = (r_idx - s_idx) * (1 - 2 * d) >= 0
    tri = seen.astype(F32)

    grow = gt_ref[...] + bcol_ref[...]
    lf_rows = _log_sigmoid(grow)
    b_rows = lax.dot_general(lf_rows, tri, (((1,), (1,)), ((), ())),
                             precision=HIGHEST, preferred_element_type=F32)
    sub = lax.broadcasted_iota(I32, grow.shape, 0)
    pick_row = lambda a, ridx: jnp.sum(jnp.where(sub == ridx, a, 0.0), axis=0, keepdims=True)
    li_row = pick_row(grow, col_i)
    b_row = pick_row(b_rows, col_f)
    cols = jnp.concatenate([li_row, b_row, jnp.zeros((LANES - 2, L), F32)], axis=0).T
    li_col = cols[:, 0:1]
    b_col = cols[:, 1:2]

    m_prev = m_ref[...]
    dmat = jnp.where(seen, b_col - b_row + li_row, -jnp.inf)
    inter = b_col + m_prev
    m_row = jnp.maximum(inter, jnp.max(dmat, axis=1, keepdims=True))
    w_inter = jnp.exp(inter - m_row)

    q = q_ref[...]
    k = k_ref[...]
    v = v_ref[...]
    qk = lax.dot_general(q, k, (((1,), (1,)), ((), ())), preferred_element_type=F32)
    s = qk * (jnp.exp(dmat - m_row) * scale)
    qf = q.astype(F32)
    num = (w_inter * jnp.dot(q, c_ref[...].astype(BF16), preferred_element_type=F32)
           + jnp.dot(s.astype(BF16), v, preferred_element_type=F32))
    den = (w_inter * jnp.sum(qf * n_ref[...], axis=1, keepdims=True)
           + jnp.sum(s, axis=1, keepdims=True))
    h_ref[...] = num / jnp.maximum(jnp.abs(den), jnp.exp(-m_row))

    b_last = jnp.max(jnp.where(fwd, b_col[L - 1:L, :], b_col[0:1, :]), axis=0, keepdims=True)
    g_col = b_last - b_col + li_col
    m_new = jnp.maximum(b_last + m_prev, jnp.max(g_col, axis=0, keepdims=True))
    decay = jnp.exp(b_last + m_prev - m_new)
    wk = jnp.exp(g_col - m_new) * scale
    kw = k.astype(F32) * wk
    c_ref[...] = decay * c_ref[...] + lax.dot_general(
        kw.astype(BF16), v, (((0,), (0,)), ((), ())), preferred_element_type=F32)
    n_ref[...] = decay * n_ref[...] + jnp.sum(kw, axis=0, keepdims=True)
    m_ref[...] = m_new


def mlstm(proj, gates_t, bias_col, *, B, S, nh, dh, L):
    nc = S // L
    T = B * S
    ng = gates_t.shape[0]

    def row(d, b, c):
        return b * nc + c + d * (nc - 1 - 2 * c)

    kern = functools.partial(_mlstm_kernel, nh=nh, scale=dh ** -0.5)
    return pl.pallas_call(
        kern,
        grid=(2, B, nh, nc),
        in_specs=[
            pl.BlockSpec((L, dh), lambda d, b, h, c: (row(d, b, c), h)),
            pl.BlockSpec((L, dh), lambda d, b, h, c: (row(d, b, c), nh + h)),
            pl.BlockSpec((L, dh), lambda d, b, h, c: (row(d, b, c), 2 * nh + h)),
            pl.BlockSpec((ng, L), lambda d, b, h, c: (0, row(d, b, c))),
            pl.BlockSpec((ng, 1), lambda d, b, h, c: (0, 0)),
        ],
        out_specs=pl.BlockSpec((None, L, dh), lambda d, b, h, c: (d, row(d, b, c), h)),
        out_shape=jax.ShapeDtypeStruct((2, T, nh * dh), F32),
        scratch_shapes=[pltpu.VMEM((dh, dh), F32), pltpu.VMEM((1, dh), F32), pltpu.VMEM((1, 1), F32)],
        compiler_params=_params(("parallel", "parallel", "parallel", "arbitrary")),
        name="mlstm",
    )(proj, proj, proj, gates_t, bias_col)


def _attn_kernel(sink_ref, q_ref, k0_ref, k1_ref, k2_ref, v0_ref, v1_ref, v2_ref,
                 bias_ref, o_ref, *, group, dh):
    nkv = k0_ref.shape[1] // dh

    for g in range(nkv):
        cols = slice(g * dh, (g + 1) * dh)
        kcat = jnp.concatenate([k0_ref[:, cols], k1_ref[:, cols], k2_ref[:, cols]], axis=0)
        vcat = jnp.concatenate([v0_ref[:, cols], v1_ref[:, cols], v2_ref[:, cols]], axis=0)
        for hq in range(g * group, (g + 1) * group):
            s = lax.dot_general(q_ref[:, hq * dh:(hq + 1) * dh], kcat, (((1,), (1,)), ((), ())),
                                preferred_element_type=F32)
            s = s + bias_ref[hq]
            sink = sink_ref[hq]
            m = jnp.maximum(jnp.max(s, axis=1, keepdims=True), sink)
            p = jnp.exp(s - m)
            den = jnp.sum(p, axis=1, keepdims=True) + jnp.exp(sink - m)
            o = jnp.dot(p.astype(BF16), vcat, preferred_element_type=F32) / den
            o_ref[:, hq * dh:(hq + 1) * dh] = o.astype(o_ref.dtype)


def attention(proj, sink, bias_tab, *, B, S, q_col, k_col, v_col, nkv, group, dh, blk):
    nb = S // blk
    T = B * S
    qw = nkv * group * dh
    kvw = nkv * dh
    assert (q_col * dh) % qw == 0 and (k_col * dh) % kvw == 0 and (v_col * dh) % kvw == 0

    def kv_spec(col0, off):
        def imap(b, i, sink_ref):
            j = jnp.clip(i + off, 0, nb - 1)
            return (b * nb + j, (col0 * dh) // kvw)
        return pl.BlockSpec((blk, kvw), imap)

    kern = functools.partial(_attn_kernel, group=group, dh=dh)
    grid_spec = pltpu.PrefetchScalarGridSpec(
        num_scalar_prefetch=1,
        grid=(B, nb),
        in_specs=[
            pl.BlockSpec((blk, qw), lambda b, i, s: (b * nb + i, (q_col * dh) // qw)),
            kv_spec(k_col, -1), kv_spec(k_col, 0), kv_spec(k_col, 1),
            kv_spec(v_col, -1), kv_spec(v_col, 0), kv_spec(v_col, 1),
            pl.BlockSpec((None, nkv * group, blk, 3 * blk),
                         lambda b, i, s: ((i > 0).astype(jnp.int32) + (i == nb - 1).astype(jnp.int32), 0, 0, 0)),
        ],
        out_specs=pl.BlockSpec((blk, qw), lambda b, i, s: (b * nb + i, 0)),
    )
    return pl.pallas_call(
        kern,
        grid_spec=grid_spec,
        out_shape=jax.ShapeDtypeStruct((T, qw), BF16),
        compiler_params=_params(("parallel", "arbitrary")),
        name="window_attn",
    )(sink, proj, proj, proj, proj, proj, proj, proj, bias_tab)


def _t5_bucket(rel):
    half = REL_BUCKETS // 2
    max_exact = half // 2
    ret = jnp.where(rel > 0, half, 0)
    n = jnp.abs(rel)
    nf = jnp.maximum(n, 1).astype(F32)
    large = max_exact + (jnp.log(nf / max_exact) / math.log(REL_MAX_DIST / max_exact)
                         * (half - max_exact)).astype(I32)
    large = jnp.minimum(large, half - 1)
    return ret + jnp.where(n < max_exact, n, large)


def attn_bias_table(rel_bias, blk, window):
    q_local = jnp.arange(blk)
    k_local = jnp.arange(3 * blk) - blk
    rel = k_local[None, :] - q_local[:, None]
    offs = jnp.arange(-(blk - 1), 3 * blk) - blk
    hit = _t5_bucket(offs)[:, None] == jnp.arange(REL_BUCKETS)
    by_off = jnp.sum(jnp.where(hit[None], rel_bias.astype(F32).T[:, None, :], 0.0), axis=-1)
    bias = jnp.stack([by_off[:, blk - 1 - q: blk - 1 - q + 3 * blk] for q in range(blk)], axis=1)
    bias = jnp.where((jnp.abs(rel) <= window)[None], bias, -jnp.inf)
    k_blk = jnp.arange(3 * blk) // blk
    first = jnp.where((k_blk == 0)[None, None], -jnp.inf, bias)
    last = jnp.where((k_blk == 2)[None, None], -jnp.inf, bias)
    return jnp.stack([first, bias, last], axis=0)


def _merge_kernel(hs_ref, mo_ref, gm_ref, ga_ref, ha_ref, x_ref, ng_ref, n2_ref,
                  wm_ref, wa_ref, wo_ref, wq_ref, x1_ref, xn2_ref, pq_ref, *, nh):
    hsum = hs_ref[0] + hs_ref[1]
    dh = hsum.shape[1] // nh
    segs = []
    for hh in range(nh):
        seg = hsum[:, hh * dh:(hh + 1) * dh]
        segs.append(seg * lax.rsqrt(jnp.mean(seg * seg, axis=-1, keepdims=True) + EPS))
    hm = jnp.concatenate(segs, axis=1) * ng_ref[...] * jax.nn.sigmoid(mo_ref[...].astype(F32))
    br_m = jnp.dot(hm.astype(BF16), wm_ref[...], preferred_element_type=F32)
    br_a = jnp.dot(ha_ref[...], wa_ref[...], preferred_element_type=F32)
    merged = (jax.nn.sigmoid(gm_ref[...].astype(F32)) * br_m
              + jax.nn.sigmoid(ga_ref[...].astype(F32)) * br_a)
    x1 = x_ref[...] + jnp.dot(merged.astype(BF16), wo_ref[...], preferred_element_type=F32)
    x1_ref[...] = x1
    xn2 = x1 * lax.rsqrt(jnp.mean(x1 * x1, axis=-1, keepdims=True) + EPS) * n2_ref[...]
    xn2_ref[...] = xn2
    pq_ref[...] = jnp.dot(xn2.astype(BF16), wq_ref[...], preferred_element_type=F32).astype(pq_ref.dtype)


def merge(hs, proj, ha, x2, ng, n2, wm, wa, wo, wq, *, nh, mo_col, gm_col, ga_col, tm):
    T, D = x2.shape
    W = hs.shape[2]
    const = lambda shape: pl.BlockSpec(shape, lambda i: (0,) * len(shape), pipeline_mode=pl.Buffered(1))
    kern = functools.partial(_merge_kernel, nh=nh)
    return pl.pallas_call(
        kern,
        grid=(T // tm,),
        in_specs=[
            pl.BlockSpec((2, tm, W), lambda i: (0, i, 0)),
            pl.BlockSpec((tm, W), lambda i: (i, mo_col)),
            pl.BlockSpec((tm, D), lambda i: (i, gm_col)),
            pl.BlockSpec((tm, D), lambda i: (i, ga_col)),
            pl.BlockSpec((tm, ha.shape[1]), lambda i: (i, 0)),
            pl.BlockSpec((tm, D), lambda i: (i, 0)),
            const((1, W)), const((1, D)),
            const(wm.shape), const(wa.shape), const(wo.shape), const(wq.shape),
        ],
        out_specs=[pl.BlockSpec((tm, D), lambda i: (i, 0))] * 3,
        out_shape=[jax.ShapeDtypeStruct((T, D), F32), jax.ShapeDtypeStruct((T, D), F32),
                   jax.ShapeDtypeStruct((T, D), BF16)],
        compiler_params=_params(("parallel",)),
        name="merge",
    )(hs, proj, proj, proj, ha, x2, ng, n2, wm, wa, wo, wq)


def _extract_top(vals, n_take, emit):
    rows = vals.shape[0]
    ridx = lax.broadcasted_iota(I32, vals.shape, 0)

    def body(kk, cur):
        mx = jnp.max(cur, axis=0, keepdims=True)
        idx = jnp.min(jnp.where(cur == mx, ridx, rows), axis=0, keepdims=True)
        hit = ridx == idx
        emit(kk, mx, idx, hit)
        return jnp.where(hit, -jnp.inf, cur)

    lax.fori_loop(0, n_take, body, vals)


def _route_kernel(pq_ref, keys_ref, e_ref, gate_ref, sv_ref, si_ref, pe_ref, *, nkeys, topk):
    tm = pq_ref.shape[0]
    nheads = keys_ref.shape[0]
    qh = pq_ref.shape[1] // (2 * nheads)
    width = [topk // (a + 1) for a in range(topk)]
    pad = pe_ref.shape[0] - sum(width)
    for h in range(nheads):
        for p in range(2):
            c0 = (2 * h + p) * qh
            sub = lax.dot_general(keys_ref[h, p], pq_ref[:, c0:c0 + qh],
                                  (((1,), (1,)), ((), ())), preferred_element_type=F32)

            def emit(kk, mx, idx, hit, p=p):
                sv_ref[pl.ds(kk, 1), p * tm:(p + 1) * tm] = mx
                si_ref[pl.ds(kk, 1), p * tm:(p + 1) * tm] = idx

            _extract_top(sub, topk, emit)

        sv0, sv1 = sv_ref[:, :tm], sv_ref[:, tm:]
        si0, si1 = si_ref[:, :tm], si_ref[:, tm:]
        cand = jnp.concatenate([sv0[a:a + 1, :] + sv1[:width[a], :] for a in range(topk)]
                               + [jnp.full((pad, tm), -jnp.inf, F32)], axis=0)
        pe_ref[...] = jnp.concatenate([si0[a:a + 1, :] * nkeys + si1[:width[a], :] for a in range(topk)]
                                      + [jnp.full((pad, tm), -1, I32)], axis=0)
        r0 = h * topk

        def emit2(kk, mx, idx, hit, r0=r0):
            gate_ref[pl.ds(r0 + kk, 1), :] = mx
            e_ref[pl.ds(r0 + kk, 1), :] = jnp.max(jnp.where(hit, pe_ref[...], -1), axis=0, keepdims=True)

        _extract_top(cand, topk, emit2)
        cv = gate_ref[r0:r0 + topk, :]
        ex = jnp.exp(cv - cv[0:1, :])
        gate_ref[r0:r0 + topk, :] = ex / jnp.sum(ex, axis=0, keepdims=True)


def _matmul_kernel(a_ref, w_ref, o_ref):
    o_ref[...] = jnp.dot(a_ref[...], w_ref[...], preferred_element_type=F32).astype(o_ref.dtype)


def matmul(a, w, *, out_dtype, tm, tn):
    T, K = a.shape
    N = w.shape[1]
    return pl.pallas_call(
        _matmul_kernel,
        grid=(T // tm, N // tn),
        in_specs=[pl.BlockSpec((tm, K), lambda i, j: (i, 0)), pl.BlockSpec((K, tn), lambda i, j: (0, j))],
        out_specs=pl.BlockSpec((tm, tn), lambda i, j: (i, j)),
        out_shape=jax.ShapeDtypeStruct((T, N), out_dtype),
        compiler_params=_params(("parallel", "parallel")),
        name="peer_query",
    )(a, w)


def peer_route(pq, keys, *, topk, tm):
    T = pq.shape[0]
    nheads, _, nkeys, kd = keys.shape
    kern = functools.partial(_route_kernel, nkeys=nkeys, topk=topk)
    ncand = sum(topk // (a + 1) for a in range(topk))
    ncand = -(-ncand // SUBLANES) * SUBLANES
    return pl.pallas_call(
        kern,
        grid=(T // tm,),
        in_specs=[
            pl.BlockSpec((tm, nheads * 2 * kd), lambda i: (i, 0)),
            pl.BlockSpec((nheads, 2, nkeys, kd), lambda i: (0, 0, 0, 0)),
        ],
        out_specs=[pl.BlockSpec((nheads * topk, tm), lambda i: (0, i)),
                   pl.BlockSpec((nheads * topk, tm), lambda i: (0, i))],
        out_shape=[jax.ShapeDtypeStruct((nheads * topk, T), I32),
                   jax.ShapeDtypeStruct((nheads * topk, T), F32)],
        scratch_shapes=[pltpu.VMEM((topk, 2 * tm), F32), pltpu.VMEM((topk, 2 * tm), I32),
                        pltpu.VMEM((ncand, tm), I32)],
        compiler_params=_params(("parallel",)),
        name="peer_route",
    )(pq, keys)


def _gelu(s):
    return 0.5 * s * (1.0 + lax.erf(s * (2.0 ** -0.5)))


def _expert_kernel(idx_cur_ref, idx_nxt_ref, tab_ref, x_ref, xn_ref, gate_ref, o_ref,
                   gbuf_a, gbuf_b, act_ref, sem_a, sem_b, *, nsel, tb, n_first):
    i = pl.program_id(0)
    nstep = pl.num_programs(0)
    rows = tb * nsel
    ngrp = nsel // SUBLANES
    nchunk = gbuf_a.shape[0]
    hi_mask = jnp.int32(-65536)
    gates = gate_ref[...]
    glane = lax.broadcasted_iota(I32, gates.shape, 1)
    tok0 = (i * 2 * tb) % gates.shape[1]

    def issue(idx_ref, idx_base, t, dst, sems, k_lo, k_hi):
        tok_idx = idx_ref.at[pl.ds(idx_base + t * nsel, nsel)]
        for kk in range(k_lo, k_hi):
            pltpu.make_async_copy(tab_ref.at[tok_idx[kk]], dst.at[:, t * nsel + kk, :],
                                  sems.at[t]).start(priority=kk % 2)

    def wait_token(buf, other, sems, t):
        rows_t = pl.ds(t * nsel, nsel)
        pltpu.make_async_copy(other.at[:, rows_t, :], buf.at[:, rows_t, :], sems.at[t]).wait()

    @pl.when(i == 0)
    def _():
        def body(t, carry):
            issue(idx_cur_ref, 0, t, gbuf_a, sem_a, 0, nsel)
            return carry
        lax.fori_loop(0, tb, body, 0)

    def phase(src, src_sems, dst, dst_sems, idx_ref, idx_base, tok_off):
        def score(t, acts):
            wait_token(src, dst, src_sems, t)
            issue(idx_ref, idx_base, t, dst, dst_sems, 0, n_first)
            xt = xn_ref[tok_off + t]
            xb = [jnp.broadcast_to(xt[c:c + 1, :], (SUBLANES, LANES)) for c in range(nchunk)]
            cols = []
            for gidx in range(ngrp):
                r0 = pl.multiple_of(t * nsel + gidx * SUBLANES, SUBLANES)
                acc = [jnp.zeros((SUBLANES, LANES), F32) for _ in range(2)]
                for c in range(nchunk):
                    uf = lax.bitcast_convert_type(src[c, pl.ds(r0, SUBLANES), :] << 16, F32)
                    acc[c % 2] = acc[c % 2] + uf * xb[c]
                cols.append(jnp.sum(acc[0] + acc[1], axis=1, keepdims=True))
            s = jnp.concatenate(cols, axis=0)
            return jnp.where(glane == tok0 + tok_off + t, s, acts)

        scores = lax.fori_loop(0, tb, score, jnp.zeros(gates.shape, F32))
        act_ref[...] = _gelu(scores) * gates

        def mix(t, carry):
            issue(idx_ref, idx_base, t, dst, dst_sems, n_first, nsel)
            a = jnp.sum(jnp.where(glane == tok0 + tok_off + t, act_ref[...], 0.0), axis=1, keepdims=True)
            accs = [jnp.zeros((SUBLANES, LANES), F32) for _ in range(nchunk)]
            for gidx in range(ngrp):
                r0 = pl.multiple_of(t * nsel + gidx * SUBLANES, SUBLANES)
                a_g = jnp.broadcast_to(a[gidx * SUBLANES:(gidx + 1) * SUBLANES, :], (SUBLANES, LANES))
                for c in range(nchunk):
                    vf = lax.bitcast_convert_type(src[c, pl.ds(r0, SUBLANES), :] & hi_mask, F32)
                    accs[c] = accs[c] + vf * a_g
            out = jnp.concatenate([jnp.sum(acc, axis=0, keepdims=True) for acc in accs], axis=0)
            o_ref[tok_off + t] = x_ref[tok_off + t] + out
            return carry

        lax.fori_loop(0, tb, mix, 0)

    phase(gbuf_a, sem_a, gbuf_b, sem_b, idx_cur_ref, rows, 0)
    phase(gbuf_b, sem_b, gbuf_a, sem_a, idx_nxt_ref, 0, tb)

    @pl.when(i == nstep - 1)
    def _():
        def body(t, carry):
            wait_token(gbuf_a, gbuf_b, sem_a, t)
            return carry
        lax.fori_loop(0, tb, body, 0)


def peer_experts(idx_flat, tab, x1_3, xn2_3, gate_t, *, nsel, tb):
    T, C, _ = x1_3.shape
    nstep = T // (2 * tb)
    rows = tb * nsel
    gblk = min(T, LANES)
    assert gblk % (2 * tb) == 0
    kern = functools.partial(_expert_kernel, nsel=nsel, tb=tb, n_first=(5 * nsel) // 8)
    return pl.pallas_call(
        kern,
        grid=(nstep,),
        in_specs=[
            pl.BlockSpec((2 * rows,), lambda i: (i,), memory_space=pltpu.SMEM),
            pl.BlockSpec((2 * rows,), lambda i: (jnp.minimum(i + 1, nstep - 1),), memory_space=pltpu.SMEM),
            pl.BlockSpec(memory_space=pl.ANY),
            pl.BlockSpec((2 * tb, C, LANES), lambda i: (i, 0, 0)),
            pl.BlockSpec((2 * tb, C, LANES), lambda i: (i, 0, 0)),
            pl.BlockSpec((nsel, gblk), lambda i: (0, (i * 2 * tb) // gblk)),
        ],
        out_specs=pl.BlockSpec((2 * tb, C, LANES), lambda i: (i, 0, 0)),
        out_shape=jax.ShapeDtypeStruct((T, C, LANES), F32),
        scratch_shapes=[pltpu.VMEM((C, rows, LANES), I32), pltpu.VMEM((C, rows, LANES), I32),
                        pltpu.VMEM((nsel, gblk), F32),
                        pltpu.SemaphoreType.DMA((tb,)), pltpu.SemaphoreType.DMA((tb,))],
        compiler_params=_params(("arbitrary",)),
        name="peer_experts",
    )(idx_flat, idx_flat, tab, x1_3, xn2_3, gate_t)


def _pack_kernel(u_ref, v_ref, o_ref):
    bu = lax.bitcast_convert_type(u_ref[...].astype(BF16).astype(F32), I32)
    bv = lax.bitcast_convert_type(v_ref[...].astype(BF16).astype(F32), I32)
    o_ref[...] = bv | lax.shift_right_logical(bu, 16)


def pack_expert_table(peer_u, peer_v, *, te=512):
    E, D = peer_u.shape
    return pl.pallas_call(
        _pack_kernel,
        grid=(E // te,),
        in_specs=[pl.BlockSpec((te, D), lambda i: (i, 0)), pl.BlockSpec((te, D), lambda i: (i, 0))],
        out_specs=pl.BlockSpec((te, D), lambda i: (i, 0)),
        out_shape=jax.ShapeDtypeStruct((E, D), I32),
        compiler_params=_params(("parallel",)),
        name="pack_experts",
    )(peer_u, peer_v)


def kernel(x, norm1_g, w_in, mlstm_gate_b, mlstm_norm_g, w_m_proj, attn_q_norm_g, attn_k_norm_g,
           attn_sink, rel_bias, w_a_proj, w_out, norm2_g, peer_wq, peer_keys, peer_u, peer_v):
    B, S, D = x.shape
    T = B * S
    mw = w_m_proj.shape[0]
    dh_m = mw // M_HEADS
    aw = w_a_proj.shape[0]
    kvw = A_KV_HEADS * A_DH
    ngate = 4 * M_HEADS

    splits = (mw, mw, mw, mw, ngate, aw, kvw, kvw, D, D)
    offs = [0]
    for sz in splits:
        offs.append(offs[-1] + sz)
    w_bf = w_in.astype(BF16)
    seg = lambda n: w_bf[:, offs[n]:offs[n + 1]]
    w_rest = jnp.concatenate([w_bf[:, :offs[4]], seg(8), seg(9), seg(5), seg(6), seg(7)], axis=1)
    wgt = w_in[:, offs[4]:offs[5]].T

    x2 = x.reshape(T, D)
    q_col = 4 * mw + 2 * D
    proj, gates_t = in_proj(x2, norm1_g.reshape(1, D), w_rest, wgt,
                            attn_q_norm_g.reshape(1, A_DH).astype(F32), attn_k_norm_g.reshape(1, A_DH).astype(F32),
                            tm=1024, tn=1024, q_col=q_col, k_col=q_col + aw, v_col=q_col + aw + kvw, dh=A_DH)

    gb = mlstm_gate_b.reshape(ngate, 1).astype(F32)
    hs = mlstm(proj, gates_t, gb, B=B, S=S, nh=M_HEADS, dh=dh_m, L=512)

    a_col0 = q_col // A_DH
    ha = attention(proj, attn_sink.astype(F32), attn_bias_table(rel_bias, WINDOW, WINDOW),
                   B=B, S=S, q_col=a_col0, k_col=a_col0 + A_HEADS, v_col=a_col0 + A_HEADS + A_KV_HEADS,
                   nkv=A_KV_HEADS, group=A_GROUP, dh=A_DH, blk=WINDOW)

    x1, xn2, pq = merge(hs, proj, ha, x2, mlstm_norm_g.reshape(1, mw), norm2_g.reshape(1, D),
                        w_m_proj.astype(BF16), w_a_proj.astype(BF16), w_out.astype(BF16), peer_wq.astype(BF16),
                        nh=M_HEADS, mo_col=3, gm_col=(4 * mw) // D, ga_col=(4 * mw) // D + 1, tm=128)

    e_t, g_t = peer_route(pq, peer_keys.astype(BF16), topk=PEER_TOPK, tm=256)
    nsel = PEER_HEADS * PEER_TOPK
    idx_flat = e_t.T.reshape(T * nsel)
    C = D // LANES
    table = pack_expert_table(peer_u, peer_v).reshape(peer_u.shape[0], C, LANES)
    y = peer_experts(idx_flat, table, x1.reshape(T, C, LANES),
                     xn2.reshape(T, C, LANES), g_t, nsel=nsel, tb=16)
    return y.reshape(B, S, D)
```
